```python
import math
import jax, jax.numpy as jnp
from jax import lax
import numpy as np

D_MODEL = 1024
BATCH = 8
SEQ = 4096
DEPTH = 2

CHUNK = 64
Q_BLOCK = 128
RMS_EPS = 1e-6
ROPE_THETA = 500000.0

DIFF_HEADS = 4
DIFF_QK_DIM = 64
DIFF_V_DIM = 2 * DIFF_QK_DIM
DIFF_ROT_DIM = DIFF_QK_DIM // 4

MLA_HEADS = 4
MLA_Q_LORA = 256
MLA_KV_LORA = 128
MLA_NOPE_DIM = 64
MLA_ROPE_DIM = 32
MLA_V_DIM = 64

SB_HEADS = 4
SB_DIM = 64

D_FF = 2816

DIFF_COLS = DIFF_HEADS * (4 * DIFF_QK_DIM + DIFF_V_DIM)
MLA_COLS = MLA_Q_LORA + MLA_KV_LORA + MLA_ROPE_DIM
SB_COLS = 3 * SB_HEADS * SB_DIM
IN_COLS = DIFF_COLS + MLA_COLS + SB_COLS
D_MIX = DIFF_HEADS * DIFF_V_DIM + MLA_HEADS * MLA_V_DIM + SB_HEADS * SB_DIM
MLA_UQ_COLS = MLA_HEADS * (MLA_NOPE_DIM + MLA_ROPE_DIM)
MLA_UKV_COLS = MLA_HEADS * (MLA_NOPE_DIM + MLA_V_DIM)

kernel_name = 'hymba_diff_mla_stickbreaking_macaron'


def rms_norm(x, g):
    xf = x.astype(jnp.float32)
    y = xf * lax.rsqrt(jnp.mean(xf * xf, axis=-1, keepdims=True) + RMS_EPS)
    return (y * g.astype(jnp.float32)).astype(x.dtype)


def swiglu(h, w_gate, w_up, w_down):
    return (jax.nn.silu(h @ w_gate) * (h @ w_up)) @ w_down


def rope_tables(rot_dim, seq):
    inv = ROPE_THETA ** (-jnp.arange(0, rot_dim, 2, dtype=jnp.float32) / rot_dim)
    ang = jnp.arange(seq, dtype=jnp.float32)[:, None] * inv[None, :]
    return jnp.cos(ang), jnp.sin(ang)


def apply_rope(x, cos, sin):
    shape = (1, cos.shape[0]) + (1,) * (x.ndim - 3) + (cos.shape[1],)
    c = cos.reshape(shape)
    s = sin.reshape(shape)
    half = x.shape[-1] // 2
    x1, x2 = x[..., :half], x[..., half:]
    return jnp.concatenate([x1 * c - x2 * s, x2 * c + x1 * s], axis=-1).astype(x.dtype)


def partial_rope(x, cos, sin):
    r = 2 * cos.shape[1]
    return jnp.concatenate([apply_rope(x[..., :r], cos, sin), x[..., r:]], axis=-1)


def sweep_query_blocks(fn, *q_arrays):
    b, s = q_arrays[0].shape[:2]
    nb = s // Q_BLOCK
    def split(a):
        return jnp.moveaxis(a.reshape((b, nb, Q_BLOCK) + a.shape[2:]), 1, 0)
    out = lax.map(lambda args: fn(args[0], *args[1:]),
                  (jnp.arange(nb),) + tuple(split(a) for a in q_arrays))
    out = jnp.moveaxis(out, 0, 1)
    return out.reshape((b, s) + out.shape[3:])


def chunk_causal_mask(block_idx, seq):
    qpos = block_idx * Q_BLOCK + jnp.arange(Q_BLOCK)
    kpos = jnp.arange(seq)
    return (kpos[None, :] // CHUNK) <= (qpos[:, None] // CHUNK)


def diff_attention(q1, q2, k1, k2, v, lam, subln, lambda_init):
    seq = k1.shape[1]
    scale = DIFF_QK_DIM ** -0.5
    def block(bi, q1b, q2b):
        allowed = chunk_causal_mask(bi, seq)
        def probs(qb, k):
            sc = jnp.einsum('bqhd,bkhd->bhqk', qb, k).astype(jnp.float32) * scale
            return jax.nn.softmax(jnp.where(allowed, sc, -jnp.inf), axis=-1)
        p = probs(q1b, k1) - lam * probs(q2b, k2)
        return jnp.einsum('bhqk,bkhd->bqhd', p.astype(v.dtype), v)
    o = sweep_query_blocks(block, q1, q2)
    return rms_norm(o, subln) * (1.0 - lambda_init)


def mla_attention(q, k_nope, k_rope, v):
    seq = k_nope.shape[1]
    scale = (MLA_NOPE_DIM + MLA_ROPE_DIM) ** -0.5
    def block(bi, qb):
        allowed = chunk_causal_mask(bi, seq)
        sc = (jnp.einsum('bqhd,bkhd->bhqk', qb[..., :MLA_NOPE_DIM], k_nope)
              + jnp.einsum('bqhd,bkd->bhqk', qb[..., MLA_NOPE_DIM:], k_rope))
        sc = sc.astype(jnp.float32) * scale
        p = jax.nn.softmax(jnp.where(allowed, sc, -jnp.inf), axis=-1)
        return jnp.einsum('bhqk,bkhd->bqhd', p.astype(v.dtype), v)
    return sweep_query_blocks(block, q)


def stick_breaking_attention(q, k, v):
    seq = k.shape[1]
    scale = SB_DIM ** -0.5
    kpos = jnp.arange(seq)
    def block(bi, qb):
        qpos = bi * Q_BLOCK + jnp.arange(Q_BLOCK)
        causal = kpos[None, :] < qpos[:, None]
        z = jnp.einsum('bqhd,bkhd->bhqk', qb, k).astype(jnp.float32) * scale
        neg_log_fail = jnp.where(causal, jax.nn.softplus(z), 0.0)
        after = lax.cumsum(neg_log_fail, axis=3, reverse=True) - neg_log_fail
        a = jnp.where(causal, jnp.exp(jax.nn.log_sigmoid(z) - after), 0.0)
        return jnp.einsum('bhqk,bkhd->bqhd', a.astype(v.dtype), v)
    return sweep_query_blocks(block, q)


def hybrid_mixer(h, layer_idx, w_in, lam_q1, lam_k1, lam_q2, lam_k2, diff_subln,
                 mla_q_norm, mla_w_uq, mla_kv_norm, mla_w_ukv, w_out,
                 cos_d, sin_d, cos_m, sin_m):
    b, s, _ = h.shape
    proj = h @ w_in
    pa = proj[..., :DIFF_COLS]
    pb = proj[..., DIFF_COLS:DIFF_COLS + MLA_COLS]
    pc = proj[..., DIFF_COLS + MLA_COLS:]

    qk_w = DIFF_HEADS * 2 * DIFF_QK_DIM
    qa = partial_rope(pa[..., :qk_w].reshape(b, s, DIFF_HEADS, 2, DIFF_QK_DIM), cos_d, sin_d)
    ka = partial_rope(pa[..., qk_w:2 * qk_w].reshape(b, s, DIFF_HEADS, 2, DIFF_QK_DIM), cos_d, sin_d)
    va = pa[..., 2 * qk_w:].reshape(b, s, DIFF_HEADS, DIFF_V_DIM)
    lambda_init = 0.8 - 0.6 * math.exp(-0.3 * layer_idx)
    lam = (jnp.exp(jnp.sum(lam_q1.astype(jnp.float32) * lam_k1.astype(jnp.float32)))
           - jnp.exp(jnp.sum(lam_q2.astype(jnp.float32) * lam_k2.astype(jnp.float32)))
           + lambda_init)
    out_a = diff_attention(qa[..., 0, :], qa[..., 1, :], ka[..., 0, :], ka[..., 1, :],
                           va, lam, diff_subln, lambda_init)

    c_q = rms_norm(pb[..., :MLA_Q_LORA], mla_q_norm)
    c_kv = rms_norm(pb[..., MLA_Q_LORA:MLA_Q_LORA + MLA_KV_LORA], mla_kv_norm)
    k_rope = apply_rope(pb[..., MLA_Q_LORA + MLA_KV_LORA:], cos_m, sin_m)
    qb = (c_q @ mla_w_uq).reshape(b, s, MLA_HEADS, MLA_NOPE_DIM + MLA_ROPE_DIM)
    qb = jnp.concatenate([qb[..., :MLA_NOPE_DIM],
                          apply_rope(qb[..., MLA_NOPE_DIM:], cos_m, sin_m)], axis=-1)
    kv = (c_kv @ mla_w_ukv).reshape(b, s, MLA_HEADS, MLA_NOPE_DIM + MLA_V_DIM)
    out_b = mla_attention(qb, kv[..., :MLA_NOPE_DIM], k_rope, kv[..., MLA_NOPE_DIM:])

    sb_w = SB_HEADS * SB_DIM
    qc = pc[..., :sb_w].reshape(b, s, SB_HEADS, SB_DIM)
    kc = pc[..., sb_w:2 * sb_w].reshape(b, s, SB_HEADS, SB_DIM)
    vc = pc[..., 2 * sb_w:].reshape(b, s, SB_HEADS, SB_DIM)
    out_c = stick_breaking_attention(qc, kc, vc)

    merged = jnp.concatenate([out_a.reshape(b, s, -1), out_b.reshape(b, s, -1),
                              out_c.reshape(b, s, -1)], axis=-1)
    return merged @ w_out


def setup_inputs(seed: int = 0):
    key = jax.random.key(seed)
    ks = jax.random.split(key, 24)
    f32 = jnp.float32
    L = DEPTH
    def nrm(k, shape, scale):
        return jax.random.normal(k, shape, f32) * scale
    def gain(k, shape):
        return 1.0 + 0.02 * jax.random.normal(k, shape, f32)
    return {
        'x': nrm(ks[0], (BATCH, SEQ, D_MODEL), 1.0),
        'ffn1_norm': gain(ks[1], (L, D_MODEL)),
        'ffn1_w_gate': nrm(ks[2], (L, D_MODEL, D_FF), D_MODEL ** -0.5),
        'ffn1_w_up': nrm(ks[3], (L, D_MODEL, D_FF), D_MODEL ** -0.5),
        'ffn1_w_down': nrm(ks[4], (L, D_FF, D_MODEL), D_FF ** -0.5),
        'mix_norm': gain(ks[5], (L, D_MODEL)),
        'w_in': nrm(ks[6], (L, D_MODEL, IN_COLS), D_MODEL ** -0.5),
        'diff_lambda_q1': nrm(ks[7], (L, DIFF_QK_DIM), 0.1),
        'diff_lambda_k1': nrm(ks[8], (L, DIFF_QK_DIM), 0.1),
        'diff_lambda_q2': nrm(ks[9], (L, DIFF_QK_DIM), 0.1),
        'diff_lambda_k2': nrm(ks[10], (L, DIFF_QK_DIM), 0.1),
        'diff_subln': gain(ks[11], (L, DIFF_V_DIM)),
        'mla_q_norm': gain(ks[12], (L, MLA_Q_LORA)),
        'mla_w_uq': nrm(ks[13], (L, MLA_Q_LORA, MLA_UQ_COLS), MLA_Q_LORA ** -0.5),
        'mla_kv_norm': gain(ks[14], (L, MLA_KV_LORA)),
        'mla_w_ukv': nrm(ks[15], (L, MLA_KV_LORA, MLA_UKV_COLS), MLA_KV_LORA ** -0.5),
        'w_out': nrm(ks[16], (L, D_MIX, D_MODEL), D_MIX ** -0.5),
        'ffn2_norm': gain(ks[17], (L, D_MODEL)),
        'ffn2_w_gate': nrm(ks[18], (L, D_MODEL, D_FF), D_MODEL ** -0.5),
        'ffn2_w_up': nrm(ks[19], (L, D_MODEL, D_FF), D_MODEL ** -0.5),
        'ffn2_w_down': nrm(ks[20], (L, D_FF, D_MODEL), D_FF ** -0.5),
        'final_norm': gain(ks[21], (D_MODEL,)),
    }


def reference(x, ffn1_norm, ffn1_w_gate, ffn1_w_up, ffn1_w_down, mix_norm, w_in,
              diff_lambda_q1, diff_lambda_k1, diff_lambda_q2, diff_lambda_k2, diff_subln,
              mla_q_norm, mla_w_uq, mla_kv_norm, mla_w_ukv, w_out,
              ffn2_norm, ffn2_w_gate, ffn2_w_up, ffn2_w_down, final_norm):
    seq = x.shape[1]
    cos_d, sin_d = rope_tables(DIFF_ROT_DIM, seq)
    cos_m, sin_m = rope_tables(MLA_ROPE_DIM, seq)
    h = x
    for i in range(DEPTH):
        h = h + 0.5 * swiglu(rms_norm(h, ffn1_norm[i]), ffn1_w_gate[i], ffn1_w_up[i], ffn1_w_down[i])
        h = h + hybrid_mixer(rms_norm(h, mix_norm[i]), i, w_in[i],
                             diff_lambda_q1[i], diff_lambda_k1[i], diff_lambda_q2[i], diff_lambda_k2[i],
                             diff_subln[i], mla_q_norm[i], mla_w_uq[i], mla_kv_norm[i], mla_w_ukv[i],
                             w_out[i], cos_d, sin_d, cos_m, sin_m)
        h = h + 0.5 * swiglu(rms_norm(h, ffn2_norm[i]), ffn2_w_gate[i], ffn2_w_up[i], ffn2_w_down[i])
    return rms_norm(h, final_norm)
```

```python
import functools
import math

import jax
import jax.numpy as jnp
from jax import lax
from jax.experimental import pallas as pl
from jax.experimental.pallas import tpu as pltpu

F32 = jnp.float32
BF16 = jnp.bfloat16

D_MODEL = 1024
CHUNK = 64
RMS_EPS = 1e-6
ROPE_THETA = 500000.0

DIFF_HEADS = 4
DIFF_QK_DIM = 64
DIFF_V_DIM = 128
DIFF_ROT_DIM = 16

MLA_HEADS = 4
MLA_Q_LORA = 256
MLA_KV_LORA = 128
MLA_NOPE_DIM = 64
MLA_ROPE_DIM = 32
MLA_V_DIM = 64

SB_HEADS = 4
SB_DIM = 64

LANES = 128
DIFF_W = DIFF_HEADS * LANES
MLA_QK_W = MLA_HEADS * LANES
MLA_V_W = MLA_HEADS * MLA_V_DIM
SB_W = SB_HEADS * SB_DIM
DIFF_COLS = 3 * DIFF_W
IN_COLS_PADDED = DIFF_COLS + MLA_Q_LORA + MLA_KV_LORA + LANES + 3 * SB_W

VMEM_LIMIT_BYTES = 56 * 1024 * 1024
LOG2E = math.log2(math.e)

ROW_TILE = 512
SOFTMAX_TILE = 512
SB_TILE = 256


def _rms(x, g):
    return x * lax.rsqrt(jnp.mean(x * x, axis=-1, keepdims=True) + RMS_EPS) * g


def _dot(a, b):
    return jnp.dot(a, b, preferred_element_type=F32)


def _dot_nt(a, b):
    return lax.dot_general(a, b, (((1,), (1,)), ((), ())), preferred_element_type=F32)


def _const_spec(shape):
    return pl.BlockSpec(shape, lambda *_: (0,) * len(shape), pipeline_mode=pl.Buffered(1))


def _params(semantics):
    return pltpu.CompilerParams(dimension_semantics=semantics, vmem_limit_bytes=VMEM_LIMIT_BYTES)


def _ffn_kernel(x_ref, g_ref, wg_ref, wu_ref, wd_ref, fn_ref, o_ref, *, ff_chunk, final_norm):
    x = x_ref[...]
    hn = _rms(x, g_ref[...]).astype(BF16)
    d_ff = wg_ref.shape[1]
    acc = jnp.zeros(x.shape, F32)
    for c in range(d_ff // ff_chunk):
        sl = slice(c * ff_chunk, (c + 1) * ff_chunk)
        gate = _dot(hn, wg_ref[:, sl])
        up = _dot(hn, wu_ref[:, sl])
        act = (gate / (1.0 + jnp.exp(-gate)) * up).astype(BF16)
        acc = acc + _dot(act, wd_ref[sl, :])
    y = x + 0.5 * acc
    if final_norm:
        y = _rms(y, fn_ref[...])
    o_ref[...] = y


def _ffn(h, g, wg, wu, wd, fn, *, tm, final_norm):
    n, d = h.shape
    d_ff = wg.shape[1]
    ff_chunk = d_ff // 2 if (d_ff // 2) % LANES == 0 else d_ff
    row = pl.BlockSpec((tm, d), lambda i: (i, 0))
    return pl.pallas_call(
        functools.partial(_ffn_kernel, ff_chunk=ff_chunk, final_norm=final_norm),
        out_shape=jax.ShapeDtypeStruct((n, d), F32),
        grid=(n // tm,),
        in_specs=[row, _const_spec((1, d)), _const_spec((d, d_ff)), _const_spec((d, d_ff)),
                  _const_spec((d_ff, d)), _const_spec((1, d))],
        out_specs=row,
        compiler_params=_params(("parallel",)),
        name="swiglu_half_step",
    )(h, g, wg, wu, wd, fn)


def _rope(x, cos, sin_up, sin_down, half):
    n = x.shape[-1]
    return x * cos + pltpu.roll(x, n - half, 1) * sin_up + pltpu.roll(x, half, 1) * sin_down


def _inproj_kernel(x_ref, g_ref, w_ref, qn_ref, kvn_ref, wuq_ref, wukv_ref,
                   dc_ref, du_ref, dd_ref, mc_ref, mu_ref, md_ref,
                   dq_ref, dk_ref, dv_ref, mq_ref, mk_ref, mv_ref, sq_ref, sk_ref, sv_ref):
    hn = _rms(x_ref[...], g_ref[...]).astype(BF16)
    dc, du, dd = dc_ref[...], du_ref[...], dd_ref[...]
    mc, mu, md = mc_ref[...], mu_ref[...], md_ref[...]
    d_half = DIFF_ROT_DIM // 2
    m_half = MLA_ROPE_DIM // 2

    q_scale = DIFF_QK_DIM ** -0.5
    for hd in range(DIFF_HEADS):
        sl = slice(hd * LANES, (hd + 1) * LANES)
        q = _dot(hn, w_ref[:, hd * LANES:(hd + 1) * LANES])
        dq_ref[:, sl] = (_rope(q, dc, du, dd, d_half) * q_scale).astype(BF16)
        k = _dot(hn, w_ref[:, DIFF_W + hd * LANES:DIFF_W + (hd + 1) * LANES])
        dk_ref[:, sl] = _rope(k, dc, du, dd, d_half).astype(BF16)
    dv_ref[...] = _dot(hn, w_ref[:, 2 * DIFF_W:3 * DIFF_W]).astype(BF16)

    c0 = DIFF_COLS
    c_q = _rms(_dot(hn, w_ref[:, c0:c0 + MLA_Q_LORA]), qn_ref[...]).astype(BF16)
    c0 += MLA_Q_LORA
    c_kv = _rms(_dot(hn, w_ref[:, c0:c0 + MLA_KV_LORA]), kvn_ref[...]).astype(BF16)
    c0 += MLA_KV_LORA
    k_rope = _rope(_dot(hn, w_ref[:, c0:c0 + LANES]), mc, mu, md, m_half)
    c0 += LANES
    for hd in range(MLA_HEADS):
        sl = slice(hd * LANES, (hd + 1) * LANES)
        q = _dot(c_q, wuq_ref[:, sl])
        mq_ref[:, sl] = _rope(q, mc, mu, md, m_half).astype(BF16)
        k_nope = _dot(c_kv, wukv_ref[:, sl])
        mk_ref[:, sl] = (k_nope + k_rope).astype(BF16)
    mv_ref[...] = _dot(c_kv, wukv_ref[:, MLA_QK_W:MLA_QK_W + MLA_V_W]).astype(BF16)

    sq_ref[...] = (_dot(hn, w_ref[:, c0:c0 + SB_W]) * (SB_DIM ** -0.5)).astype(BF16)
    sk_ref[...] = _dot(hn, w_ref[:, c0 + SB_W:c0 + 2 * SB_W]).astype(BF16)
    sv_ref[...] = _dot(hn, w_ref[:, c0 + 2 * SB_W:c0 + 3 * SB_W]).astype(BF16)


def _inproj(h, g, w, qn, kvn, wuq, wukv, diff_tabs, mla_tabs, *, tm, seq):
    n, d = h.shape
    pos_blocks = seq // tm
    row = pl.BlockSpec((tm, d), lambda i: (i, 0))
    tab = pl.BlockSpec((tm, LANES), lambda i: (i % pos_blocks, 0))
    widths = (DIFF_W, DIFF_W, DIFF_W, MLA_QK_W, MLA_QK_W, MLA_V_W, SB_W, SB_W, SB_W)
    return pl.pallas_call(
        _inproj_kernel,
        out_shape=tuple(jax.ShapeDtypeStruct((n, wd), BF16) for wd in widths),
        grid=(n // tm,),
        in_specs=[row, _const_spec((1, d)), _const_spec(w.shape), _const_spec(qn.shape),
                  _const_spec(kvn.shape), _const_spec(wuq.shape), _const_spec(wukv.shape)] + [tab] * 6,
        out_specs=tuple(pl.BlockSpec((tm, wd), lambda i: (i, 0)) for wd in widths),
        compiler_params=_params(("parallel",)),
        name="mixer_in_proj",
    )(h, g, w, qn, kvn, wuq, wukv, *diff_tabs, *mla_tabs)


def _chunk_mask(t):
    row = lax.broadcasted_iota(jnp.int32, (t, t), 0)
    col = lax.broadcasted_iota(jnp.int32, (t, t), 1)
    return (col // CHUNK) <= (row // CHUNK)


def _softmax_update(s, m, l, acc_ref, v, c):
    m_new = jnp.maximum(m, jnp.max(s, axis=-1, keepdims=True))
    alpha = jnp.exp2((m - m_new) * c)
    p = jnp.exp2((s - m_new) * c)
    l_new = alpha * l + jnp.sum(p, axis=-1, keepdims=True)
    acc_ref[...] = alpha * acc_ref[...] + _dot(p.astype(BF16), v)
    return m_new, l_new


def _diff_kernel(lam_ref, subln_ref, q_ref, k_ref, v_ref, o_ref, acc1_ref, acc2_ref, *, t, lambda_init):
    qi = pl.program_id(2)
    q = q_ref[...]
    lane = lax.broadcasted_iota(jnp.int32, q.shape, 1)
    q1 = jnp.where(lane < DIFF_QK_DIM, q, jnp.zeros_like(q))
    q2 = jnp.where(lane >= DIFF_QK_DIM, q, jnp.zeros_like(q))
    acc1_ref[...] = jnp.zeros_like(acc1_ref)
    acc2_ref[...] = jnp.zeros_like(acc2_ref)

    def tile(kj, carry, mask):
        m1, l1, m2, l2 = carry
        start = pl.multiple_of(kj * t, t)
        k = k_ref[pl.ds(start, t), :]
        v = v_ref[pl.ds(start, t), :]
        s1 = _dot_nt(q1, k)
        s2 = _dot_nt(q2, k)
        if mask is not None:
            s1 = jnp.where(mask, s1, -jnp.inf)
            s2 = jnp.where(mask, s2, -jnp.inf)
        m1, l1 = _softmax_update(s1, m1, l1, acc1_ref, v, LOG2E)
        m2, l2 = _softmax_update(s2, m2, l2, acc2_ref, v, LOG2E)
        return m1, l1, m2, l2

    neg = jnp.full((t, 1), -jnp.inf, F32)
    zero = jnp.zeros((t, 1), F32)
    carry = lax.fori_loop(0, qi, lambda kj, c: tile(kj, c, None), (neg, zero, neg, zero))
    m1, l1, m2, l2 = tile(qi, carry, _chunk_mask(t))

    lam_p = lam_ref[...]
    lam = (jnp.exp(jnp.sum(lam_p[0:1] * lam_p[1:2], axis=-1, keepdims=True))
           - jnp.exp(jnp.sum(lam_p[2:3] * lam_p[3:4], axis=-1, keepdims=True)) + lambda_init)
    o = acc1_ref[...] / l1 - lam * (acc2_ref[...] / l2)
    o_ref[...] = (_rms(o, subln_ref[...]) * (1.0 - lambda_init)).astype(o_ref.dtype)


def _diff_attention(lam_p, subln, q, k, v, *, t, lambda_init):
    b, s, _ = q.shape
    qspec = pl.BlockSpec((None, t, LANES), lambda bi, hi, qi: (bi, qi, hi))
    kvspec = pl.BlockSpec((None, s, LANES), lambda bi, hi, qi: (bi, 0, hi))
    return pl.pallas_call(
        functools.partial(_diff_kernel, t=t, lambda_init=lambda_init),
        out_shape=jax.ShapeDtypeStruct((b, s, DIFF_W), BF16),
        grid=(b, DIFF_HEADS, s // t),
        in_specs=[_const_spec(lam_p.shape), _const_spec(subln.shape), qspec, kvspec, kvspec],
        out_specs=qspec,
        scratch_shapes=[pltpu.VMEM((t, LANES), F32), pltpu.VMEM((t, LANES), F32)],
        compiler_params=_params(("parallel", "parallel", "arbitrary")),
        name="diff_attention",
    )(lam_p, subln, q, k, v)


def _mla_kernel(q_ref, k_ref, v_ref, o_ref, acc_ref, *, t):
    qi = pl.program_id(2)
    c = (MLA_NOPE_DIM + MLA_ROPE_DIM) ** -0.5 * LOG2E
    mask = _chunk_mask(t)
    outs = []
    for hd in range(2):
        lanes = slice(hd * LANES, (hd + 1) * LANES)
        q = q_ref[:, lanes]
        acc_ref[...] = jnp.zeros_like(acc_ref)

        def tile(kj, carry, msk, q=q, lanes=lanes):
            m, l = carry
            start = pl.multiple_of(kj * t, t)
            s = _dot_nt(q, k_ref[pl.ds(start, t), lanes])
            if msk is not None:
                s = jnp.where(msk, s, -jnp.inf)
            return _softmax_update(s, m, l, acc_ref, v_ref[pl.ds(start, t), :], c)

        init = (jnp.full((t, 1), -jnp.inf, F32), jnp.zeros((t, 1), F32))
        carry = lax.fori_loop(0, qi, lambda kj, cr: tile(kj, cr, None), init)
        _, l = tile(qi, carry, mask)
        outs.append(acc_ref[...] / l)
    lane = lax.broadcasted_iota(jnp.int32, outs[0].shape, 1)
    o_ref[...] = jnp.where(lane < MLA_V_DIM, outs[0], outs[1]).astype(o_ref.dtype)


def _mla_attention(q, k, v, *, t):
    b, s, _ = q.shape
    pairs = MLA_HEADS // 2
    return pl.pallas_call(
        functools.partial(_mla_kernel, t=t),
        out_shape=jax.ShapeDtypeStruct((b, s, MLA_V_W), BF16),
        grid=(b, pairs, s // t),
        in_specs=[pl.BlockSpec((None, t, 2 * LANES), lambda bi, pi, qi: (bi, qi, pi)),
                  pl.BlockSpec((None, s, 2 * LANES), lambda bi, pi, qi: (bi, 0, pi)),
                  pl.BlockSpec((None, s, LANES), lambda bi, pi, qi: (bi, 0, pi))],
        out_specs=pl.BlockSpec((None, t, LANES), lambda bi, pi, qi: (bi, qi, pi)),
        scratch_shapes=[pltpu.VMEM((t, LANES), F32)],
        compiler_params=_params(("parallel", "parallel", "arbitrary")),
        name="mla_attention",
    )(q, k, v)


def _sb_kernel(q_ref, k_ref, v_ref, o_ref, acc_ref, *, t):
    qi = pl.program_id(2)
    row = lax.broadcasted_iota(jnp.int32, (t, t), 0)
    col = lax.broadcasted_iota(jnp.int32, (t, t), 1)
    causal = col < row
    after_mat = jnp.where(row > col, 1.0, 0.0).astype(BF16)
    q_all = q_ref[...]
    lane = lax.broadcasted_iota(jnp.int32, q_all.shape, 1)
    outs = []
    for hd in range(2):
        in_head = (lane >= hd * SB_DIM) & (lane < (hd + 1) * SB_DIM)
        q = jnp.where(in_head, q_all, jnp.zeros_like(q_all))
        acc_ref[...] = jnp.zeros_like(acc_ref)

        def tile(kj, carry, msk, q=q):
            start = pl.multiple_of(kj * t, t)
            z = _dot_nt(q, k_ref[pl.ds(start, t), :])
            tail = jnp.log(1.0 + jnp.exp(-jnp.abs(z)))
            softplus = jnp.maximum(z, 0.0) + tail
            log_sig = jnp.minimum(z, 0.0) - tail
            if msk is not None:
                softplus = jnp.where(msk, softplus, 0.0)
            hi = softplus.astype(BF16)
            lo = (softplus - hi.astype(F32)).astype(BF16)
            after = _dot(hi, after_mat) + _dot(lo, after_mat) + carry
            a = jnp.exp(log_sig - after)
            if msk is not None:
                a = jnp.where(msk, a, 0.0)
            acc_ref[...] += _dot(a.astype(BF16), v_ref[pl.ds(start, t), :])
            return carry + jnp.sum(softplus, axis=-1, keepdims=True)

        carry = tile(qi, jnp.zeros((t, 1), F32), causal)
        lax.fori_loop(0, qi, lambda it, cr: tile(qi - 1 - it, cr, None), carry)
        outs.append(acc_ref[...])
    o_ref[...] = jnp.where(lane < SB_DIM, outs[0], outs[1]).astype(o_ref.dtype)


def _sb_attention(q, k, v, *, t):
    b, s, _ = q.shape
    pairs = SB_HEADS // 2
    qspec = pl.BlockSpec((None, t, LANES), lambda bi, pi, qi: (bi, qi, pi))
    kvspec = pl.BlockSpec((None, s, LANES), lambda bi, pi, qi: (bi, 0, pi))
    return pl.pallas_call(
        functools.partial(_sb_kernel, t=t),
        out_shape=jax.ShapeDtypeStruct((b, s, SB_W), BF16),
        grid=(b, pairs, s // t),
        in_specs=[qspec, kvspec, kvspec],
        out_specs=qspec,
        scratch_shapes=[pltpu.VMEM((t, LANES), F32)],
        compiler_params=_params(("parallel", "parallel", "arbitrary")),
        name="stick_breaking_attention",
    )(q, k, v)


def _outproj_kernel(x_ref, a_ref, b_ref, c_ref, w_ref, o_ref):
    wa = DIFF_W
    wb = wa + MLA_V_W
    o_ref[...] = (x_ref[...] + _dot(a_ref[...], w_ref[0:wa, :]) + _dot(b_ref[...], w_ref[wa:wb, :])
                  + _dot(c_ref[...], w_ref[wb:wb + SB_W, :]))


def _outproj(h, oa, ob, oc, w, *, tm):
    n, d = h.shape
    row = lambda wd: pl.BlockSpec((tm, wd), lambda i: (i, 0))
    return pl.pallas_call(
        _outproj_kernel,
        out_shape=jax.ShapeDtypeStruct((n, d), F32),
        grid=(n // tm,),
        in_specs=[row(d), row(DIFF_W), row(MLA_V_W), row(SB_W), _const_spec(w.shape)],
        out_specs=row(d),
        compiler_params=_params(("parallel",)),
        name="mixer_out_proj",
    )(h, oa, ob, oc, w)


def _rope_tables(rot_dim, seq, lane_starts):
    half = rot_dim // 2
    inv = ROPE_THETA ** (-jnp.arange(0, rot_dim, 2, dtype=F32) / rot_dim)
    ang = jnp.arange(seq, dtype=F32)[:, None] * inv[None, :]
    cos, sin = jnp.cos(ang), jnp.sin(ang)
    tc = jnp.ones((seq, LANES), F32)
    tu = jnp.zeros((seq, LANES), F32)
    td = jnp.zeros((seq, LANES), F32)
    for st in lane_starts:
        tc = tc.at[:, st:st + half].set(cos).at[:, st + half:st + 2 * half].set(cos)
        tu = tu.at[:, st:st + half].set(-sin)
        td = td.at[:, st + half:st + 2 * half].set(sin)
    return tc, tu, td


def _pack_in_weights(w_in, w_uq, w_ukv):
    d = w_in.shape[0]
    c_rope = DIFF_COLS + MLA_Q_LORA + MLA_KV_LORA
    rope_slab = jnp.zeros((d, LANES), F32).at[:, MLA_NOPE_DIM:MLA_NOPE_DIM + MLA_ROPE_DIM].set(
        w_in[:, c_rope:c_rope + MLA_ROPE_DIM])
    w = jnp.concatenate([w_in[:, :c_rope], rope_slab, w_in[:, c_rope + MLA_ROPE_DIM:]], axis=1)
    qd = MLA_NOPE_DIM + MLA_ROPE_DIM
    uq = jnp.pad(w_uq.reshape(MLA_Q_LORA, MLA_HEADS, qd), ((0, 0), (0, 0), (0, LANES - qd)))
    ukv = w_ukv.reshape(MLA_KV_LORA, MLA_HEADS, MLA_NOPE_DIM + MLA_V_DIM)
    uk = jnp.pad(ukv[:, :, :MLA_NOPE_DIM], ((0, 0), (0, 0), (0, LANES - MLA_NOPE_DIM)))
    uv = ukv[:, :, MLA_NOPE_DIM:]
    ukv_packed = jnp.concatenate([uk.reshape(MLA_KV_LORA, MLA_QK_W), uv.reshape(MLA_KV_LORA, MLA_V_W)], axis=1)
    return w.astype(BF16), uq.reshape(MLA_Q_LORA, MLA_QK_W).astype(BF16), ukv_packed.astype(BF16)


def kernel(x, ffn1_norm, ffn1_w_gate, ffn1_w_up, ffn1_w_down, mix_norm, w_in, diff_lambda_q1, diff_lambda_k1, diff_lambda_q2, diff_lambda_k2, diff_subln, mla_q_norm, mla_w_uq, mla_kv_norm, mla_w_ukv, w_out, ffn2_norm, ffn2_w_gate, ffn2_w_up, ffn2_w_down, final_norm):
    b, s, d = x.shape
    depth = w_in.shape[0]
    tm = min(ROW_TILE, s)
    t_soft = min(SOFTMAX_TILE, s)
    t_sb = min(SB_TILE, s)
    diff_tabs = _rope_tables(DIFF_ROT_DIM, s, (0, DIFF_QK_DIM))
    mla_tabs = _rope_tables(MLA_ROPE_DIM, s, (MLA_NOPE_DIM,))
    fn = final_norm.reshape(1, d)
    h = x.reshape(b * s, d)
    for i in range(depth):
        h = _ffn(h, ffn1_norm[i].reshape(1, d), ffn1_w_gate[i].astype(BF16), ffn1_w_up[i].astype(BF16),
                 ffn1_w_down[i].astype(BF16), fn, tm=tm, final_norm=False)
        w, uq, ukv = _pack_in_weights(w_in[i], mla_w_uq[i], mla_w_ukv[i])
        dq, dk, dv, mq, mk, mv, sq, sk, sv = _inproj(
            h, mix_norm[i].reshape(1, d), w, mla_q_norm[i].reshape(1, -1), mla_kv_norm[i].reshape(1, -1),
            uq, ukv, diff_tabs, mla_tabs, tm=tm, seq=s)
        shape3 = lambda a: a.reshape(b, s, a.shape[-1])
        lam_p = jnp.stack([diff_lambda_q1[i], diff_lambda_k1[i], diff_lambda_q2[i], diff_lambda_k2[i]])
        lambda_init = 0.8 - 0.6 * math.exp(-0.3 * i)
        oa = _diff_attention(lam_p, diff_subln[i].reshape(1, -1), shape3(dq), shape3(dk), shape3(dv),
                             t=t_soft, lambda_init=lambda_init)
        ob = _mla_attention(shape3(mq), shape3(mk), shape3(mv), t=t_soft)
        oc = _sb_attention(shape3(sq), shape3(sk), shape3(sv), t=t_sb)
        flat = lambda a: a.reshape(b * s, a.shape[-1])
        h = _outproj(h, flat(oa), flat(ob), flat(oc), w_out[i].astype(BF16), tm=tm)
        h = _ffn(h, ffn2_norm[i].reshape(1, d), ffn2_w_gate[i].astype(BF16), ffn2_w_up[i].astype(BF16),
                 ffn2_w_down[i].astype(BF16), fn, tm=tm, final_norm=(i == depth - 1))
    return h.reshape(b, s, d)
```

```python
import functools
import math

import jax
import jax.numpy as jnp
from jax import lax
from jax.experimental import pallas as pl
from jax.experimental.pallas import tpu as pltpu

F32 = jnp.float32
BF16 = jnp.bfloat16

D_MODEL = 1024
CHUNK = 64
RMS_EPS = 1e-6
ROPE_THETA = 500000.0

DIFF_HEADS = 4
DIFF_QK_DIM = 64
DIFF_V_DIM = 128
DIFF_ROT_DIM = 16

MLA_HEADS = 4
MLA_Q_LORA = 256
MLA_KV_LORA = 128
MLA_NOPE_DIM = 64
MLA_ROPE_DIM = 32
MLA_V_DIM = 64

SB_HEADS = 4
SB_DIM = 64

LANES = 128
DIFF_W = DIFF_HEADS * LANES
MLA_QK_W = MLA_HEADS * LANES
MLA_V_W = MLA_HEADS * MLA_V_DIM
SB_W = SB_HEADS * SB_DIM
DIFF_COLS = 3 * DIFF_W
IN_COLS_PADDED = DIFF_COLS + MLA_Q_LORA + MLA_KV_LORA + LANES + 3 * SB_W

VMEM_LIMIT_BYTES = 56 * 1024 * 1024

ROW_TILE = 512
SOFTMAX_TILE = 512
SB_TILE = 256


def _rms(x, g):
    return x * lax.rsqrt(jnp.mean(x * x, axis=-1, keepdims=True) + RMS_EPS) * g


def _dot(a, b):
    return jnp.dot(a, b, preferred_element_type=F32)


def _dot_nt(a, b):
    return lax.dot_general(a, b, (((1,), (1,)), ((), ())), preferred_element_type=F32)


def _const_spec(shape):
    return pl.BlockSpec(shape, lambda *_: (0,) * len(shape), pipeline_mode=pl.Buffered(1))


def _params(semantics):
    return pltpu.CompilerParams(dimension_semantics=semantics, vmem_limit_bytes=VMEM_LIMIT_BYTES)


def _ffn_kernel(x_ref, g_ref, wg_ref, wu_ref, wd_ref, fn_ref, o_ref, *, ff_chunk, final_norm):
    x = x_ref[...]
    hn = _rms(x, g_ref[...]).astype(BF16)
    d_ff = wg_ref.shape[1]
    acc = jnp.zeros(x.shape, F32)
    for c in range(d_ff // ff_chunk):
        sl = slice(c * ff_chunk, (c + 1) * ff_chunk)
        gate = _dot(hn, wg_ref[:, sl])
        up = _dot(hn, wu_ref[:, sl])
        act = (gate / (1.0 + jnp.exp(-gate)) * up).astype(BF16)
        acc = acc + _dot(act, wd_ref[sl, :])
    y = x + 0.5 * acc
    if final_norm:
        y = _rms(y, fn_ref[...])
    o_ref[...] = y


def _ffn(h, g, wg, wu, wd, fn, *, tm, final_norm):
    n, d = h.shape
    d_ff = wg.shape[1]
    ff_chunk = d_ff // 2 if (d_ff // 2) % LANES == 0 else d_ff
    row = pl.BlockSpec((tm, d), lambda i: (i, 0))
    return pl.pallas_call(
        functools.partial(_ffn_kernel, ff_chunk=ff_chunk, final_norm=final_norm),
        out_shape=jax.ShapeDtypeStruct((n, d), F32),
        grid=(n // tm,),
        in_specs=[row, _const_spec((1, d)), _const_spec((d, d_ff)), _const_spec((d, d_ff)),
                  _const_spec((d_ff, d)), _const_spec((1, d))],
        out_specs=row,
        compiler_params=_params(("parallel",)),
        name="swiglu_half_step",
    )(h, g, wg, wu, wd, fn)


def _rope(x, cos, sin_up, sin_down, half):
    n = x.shape[-1]
    return x * cos + pltpu.roll(x, n - half, 1) * sin_up + pltpu.roll(x, half, 1) * sin_down


def _inproj_kernel(x_ref, g_ref, w_ref, qn_ref, kvn_ref, wuq_ref, wukv_ref,
                   dc_ref, du_ref, dd_ref, mc_ref, mu_ref, md_ref,
                   dq_ref, dk_ref, dv_ref, mq_ref, mk_ref, mv_ref, sq_ref, sk_ref, sv_ref):
    hn = _rms(x_ref[...], g_ref[...]).astype(BF16)
    dc, du, dd = dc_ref[...], du_ref[...], dd_ref[...]
    mc, mu, md = mc_ref[...], mu_ref[...], md_ref[...]
    d_half = DIFF_ROT_DIM // 2
    m_half = MLA_ROPE_DIM // 2

    q_scale = DIFF_QK_DIM ** -0.5
    for hd in range(DIFF_HEADS):
        sl = slice(hd * LANES, (hd + 1) * LANES)
        q = _dot(hn, w_ref[:, hd * LANES:(hd + 1) * LANES])
        dq_ref[:, sl] = (_rope(q, dc, du, dd, d_half) * q_scale).astype(BF16)
        k = _dot(hn, w_ref[:, DIFF_W + hd * LANES:DIFF_W + (hd + 1) * LANES])
        dk_ref[:, sl] = _rope(k, dc, du, dd, d_half).astype(BF16)
    dv_ref[...] = _dot(hn, w_ref[:, 2 * DIFF_W:3 * DIFF_W]).astype(BF16)

    mla_scale = (MLA_NOPE_DIM + MLA_ROPE_DIM) ** -0.5
    c0 = DIFF_COLS
    c_q = _rms(_dot(hn, w_ref[:, c0:c0 + MLA_Q_LORA]), qn_ref[...]).astype(BF16)
    c0 += MLA_Q_LORA
    c_kv = _rms(_dot(hn, w_ref[:, c0:c0 + MLA_KV_LORA]), kvn_ref[...]).astype(BF16)
    c0 += MLA_KV_LORA
    k_rope = _rope(_dot(hn, w_ref[:, c0:c0 + LANES]), mc, mu, md, m_half)
    c0 += LANES
    for hd in range(MLA_HEADS):
        sl = slice(hd * LANES, (hd + 1) * LANES)
        q = _dot(c_q, wuq_ref[:, sl])
        mq_ref[:, sl] = (_rope(q, mc, mu, md, m_half) * mla_scale).astype(BF16)
        k_nope = _dot(c_kv, wukv_ref[:, sl])
        mk_ref[:, sl] = (k_nope + k_rope).astype(BF16)
    mv_ref[...] = _dot(c_kv, wukv_ref[:, MLA_QK_W:MLA_QK_W + MLA_V_W]).astype(BF16)

    sq_ref[...] = (_dot(hn, w_ref[:, c0:c0 + SB_W]) * (SB_DIM ** -0.5)).astype(BF16)
    sk_ref[...] = _dot(hn, w_ref[:, c0 + SB_W:c0 + 2 * SB_W]).astype(BF16)
    sv_ref[...] = _dot(hn, w_ref[:, c0 + 2 * SB_W:c0 + 3 * SB_W]).astype(BF16)


def _inproj(h, g, w, qn, kvn, wuq, wukv, diff_tabs, mla_tabs, *, tm, seq):
    n, d = h.shape
    pos_blocks = seq // tm
    row = pl.BlockSpec((tm, d), lambda i: (i, 0))
    tab = pl.BlockSpec((tm, LANES), lambda i: (i % pos_blocks, 0))
    widths = (DIFF_W, DIFF_W, DIFF_W, MLA_QK_W, MLA_QK_W, MLA_V_W, SB_W, SB_W, SB_W)
    return pl.pallas_call(
        _inproj_kernel,
        out_shape=tuple(jax.ShapeDtypeStruct((n, wd), BF16) for wd in widths),
        grid=(n // tm,),
        in_specs=[row, _const_spec((1, d)), _const_spec(w.shape), _const_spec(qn.shape),
                  _const_spec(kvn.shape), _const_spec(wuq.shape), _const_spec(wukv.shape)] + [tab] * 6,
        out_specs=tuple(pl.BlockSpec((tm, wd), lambda i: (i, 0)) for wd in widths),
        compiler_params=_params(("parallel",)),
        name="mixer_in_proj",
    )(h, g, w, qn, kvn, wuq, wukv, *diff_tabs, *mla_tabs)


def _chunk_mask(t):
    row = lax.broadcasted_iota(jnp.int32, (t, t), 0)
    col = lax.broadcasted_iota(jnp.int32, (t, t), 1)
    return (col // CHUNK) <= (row // CHUNK)


def _softmax_update(s, m, l, acc_ref, v):
    m_new = jnp.maximum(m, jnp.max(s, axis=-1, keepdims=True))
    alpha = jnp.exp(m - m_new)
    p = jnp.exp(s - m_new)
    l_new = alpha * l + jnp.sum(p, axis=-1, keepdims=True)
    acc_ref[...] = alpha * acc_ref[...] + _dot(p.astype(BF16), v)
    return m_new, l_new


def _diff_kernel(lam_ref, subln_ref, q_ref, k_ref, v_ref, o_ref, acc1_ref, acc2_ref, *, t, lambda_init):
    qi = pl.program_id(2)
    q = q_ref[...]
    lane = lax.broadcasted_iota(jnp.int32, q.shape, 1)
    q1 = jnp.where(lane < DIFF_QK_DIM, q, jnp.zeros_like(q))
    q2 = jnp.where(lane >= DIFF_QK_DIM, q, jnp.zeros_like(q))
    acc1_ref[...] = jnp.zeros_like(acc1_ref)
    acc2_ref[...] = jnp.zeros_like(acc2_ref)

    def tile(kj, carry, mask):
        m1, l1, m2, l2 = carry
        start = pl.multiple_of(kj * t, t)
        k = k_ref[pl.ds(start, t), :]
        v = v_ref[pl.ds(start, t), :]
        s1 = _dot_nt(q1, k)
        s2 = _dot_nt(q2, k)
        if mask is not None:
            s1 = jnp.where(mask, s1, -jnp.inf)
            s2 = jnp.where(mask, s2, -jnp.inf)
        m1, l1 = _softmax_update(s1, m1, l1, acc1_ref, v)
        m2, l2 = _softmax_update(s2, m2, l2, acc2_ref, v)
        return m1, l1, m2, l2

    neg = jnp.full((t, 1), -jnp.inf, F32)
    zero = jnp.zeros((t, 1), F32)
    carry = lax.fori_loop(0, qi, lambda kj, c: tile(kj, c, None), (neg, zero, neg, zero))
    m1, l1, m2, l2 = tile(qi, carry, _chunk_mask(t))

    lam_p = lam_ref[...]
    lam = (jnp.exp(jnp.sum(lam_p[0:1] * lam_p[1:2], axis=-1, keepdims=True))
           - jnp.exp(jnp.sum(lam_p[2:3] * lam_p[3:4], axis=-1, keepdims=True)) + lambda_init)
    o = acc1_ref[...] / l1 - lam * (acc2_ref[...] / l2)
    o_ref[...] = (_rms(o, subln_ref[...]) * (1.0 - lambda_init)).astype(o_ref.dtype)


def _diff_attention(lam_p, subln, q, k, v, *, t, lambda_init):
    b, s, _ = q.shape
    qspec = pl.BlockSpec((None, t, LANES), lambda bi, hi, qi: (bi, qi, hi))
    kvspec = pl.BlockSpec((None, s, LANES), lambda bi, hi, qi: (bi, 0, hi))
    return pl.pallas_call(
        functools.partial(_diff_kernel, t=t, lambda_init=lambda_init),
        out_shape=jax.ShapeDtypeStruct((b, s, DIFF_W), BF16),
        grid=(b, DIFF_HEADS, s // t),
        in_specs=[_const_spec(lam_p.shape), _const_spec(subln.shape), qspec, kvspec, kvspec],
        out_specs=qspec,
        scratch_shapes=[pltpu.VMEM((t, LANES), F32), pltpu.VMEM((t, LANES), F32)],
        compiler_params=_params(("parallel", "parallel", "arbitrary")),
        name="diff_attention",
    )(lam_p, subln, q, k, v)


def _mla_kernel(q_ref, k_ref, v_ref, o_ref, acc_ref, *, t):
    qi = pl.program_id(2)
    lane = lax.broadcasted_iota(jnp.int32, (t, LANES), 1)
    first = lane < MLA_V_DIM
    q0 = q_ref[:, 0:LANES]
    q1 = q_ref[:, LANES:2 * LANES]

    def probs(s, m, l):
        m_new = jnp.maximum(m, jnp.max(s, axis=-1, keepdims=True))
        alpha = jnp.exp(m - m_new)
        p = jnp.exp(s - m_new)
        return p.astype(BF16), alpha, m_new, alpha * l + jnp.sum(p, axis=-1, keepdims=True)

    def tile(kj, carry, msk):
        m0, l0, m1, l1 = carry
        start = pl.multiple_of(kj * t, t)
        s0 = _dot_nt(q0, k_ref[pl.ds(start, t), 0:LANES])
        s1 = _dot_nt(q1, k_ref[pl.ds(start, t), LANES:2 * LANES])
        if msk is not None:
            s0 = jnp.where(msk, s0, -jnp.inf)
            s1 = jnp.where(msk, s1, -jnp.inf)
        p0, alpha0, m0, l0 = probs(s0, m0, l0)
        p1, alpha1, m1, l1 = probs(s1, m1, l1)
        v = v_ref[pl.ds(start, t), :]
        vz = jnp.zeros_like(v)
        pv = _dot(p0, jnp.where(first, v, vz)) + _dot(p1, jnp.where(first, vz, v))
        acc_ref[...] = jnp.where(first, alpha0, alpha1) * acc_ref[...] + pv
        return m0, l0, m1, l1

    acc_ref[...] = jnp.zeros_like(acc_ref)
    neg = jnp.full((t, 1), -jnp.inf, F32)
    zero = jnp.zeros((t, 1), F32)
    carry = lax.fori_loop(0, qi, lambda kj, cr: tile(kj, cr, None), (neg, zero, neg, zero))
    _, l0, _, l1 = tile(qi, carry, _chunk_mask(t))
    o_ref[...] = (acc_ref[...] / jnp.where(first, l0, l1)).astype(o_ref.dtype)


def _mla_attention(q, k, v, *, t):
    b, s, _ = q.shape
    pairs = MLA_HEADS // 2
    return pl.pallas_call(
        functools.partial(_mla_kernel, t=t),
        out_shape=jax.ShapeDtypeStruct((b, s, MLA_V_W), BF16),
        grid=(b, pairs, s // t),
        in_specs=[pl.BlockSpec((None, t, 2 * LANES), lambda bi, pi, qi: (bi, qi, pi)),
                  pl.BlockSpec((None, s, 2 * LANES), lambda bi, pi, qi: (bi, 0, pi)),
                  pl.BlockSpec((None, s, LANES), lambda bi, pi, qi: (bi, 0, pi))],
        out_specs=pl.BlockSpec((None, t, LANES), lambda bi, pi, qi: (bi, qi, pi)),
        scratch_shapes=[pltpu.VMEM((t, LANES), F32)],
        compiler_params=_params(("parallel", "parallel", "arbitrary")),
        name="mla_attention",
    )(q, k, v)


def _sb_kernel(q_ref, k_ref, v_ref, o_ref, acc_ref, *, t):
    qi = pl.program_id(2)
    row = lax.broadcasted_iota(jnp.int32, (t, t), 0)
    col = lax.broadcasted_iota(jnp.int32, (t, t), 1)
    causal = col < row
    after_mat = jnp.where(row > col, 1.0, 0.0).astype(BF16)
    q_all = q_ref[...]
    lane = lax.broadcasted_iota(jnp.int32, q_all.shape, 1)
    first = lane < SB_DIM
    zeros = jnp.zeros_like(q_all)
    qs = (jnp.where(first, q_all, zeros), jnp.where(first, zeros, q_all))
    sign_bit = jnp.uint32(0x80000000)

    def weights(q, k, msk):
        z = _dot_nt(q, k)
        neg_abs = lax.bitcast_convert_type(lax.bitcast_convert_type(z, jnp.uint32) | sign_bit, F32)
        softplus = jnp.maximum(z, 0.0) + jnp.log(1.0 + jnp.exp(neg_abs))
        log_sig = z - softplus
        if msk is not None:
            softplus = jnp.where(msk, softplus, 0.0)
        a = jnp.exp(log_sig - _dot(softplus.astype(BF16), after_mat))
        if msk is not None:
            a = jnp.where(msk, a, 0.0)
        return a.astype(BF16), jnp.sum(softplus, axis=-1, keepdims=True)

    def tile(kj, carry, msk):
        c0, c1 = carry
        start = pl.multiple_of(kj * t, t)
        k = k_ref[pl.ds(start, t), :]
        v = v_ref[pl.ds(start, t), :]
        vz = jnp.zeros_like(v)
        a0, r0 = weights(qs[0], k, msk)
        a1, r1 = weights(qs[1], k, msk)
        pv = _dot(a0, jnp.where(first, v, vz)) + _dot(a1, jnp.where(first, vz, v))
        acc_ref[...] += pv * jnp.where(first, jnp.exp(-c0), jnp.exp(-c1))
        return c0 + r0, c1 + r1

    acc_ref[...] = jnp.zeros_like(acc_ref)
    zero = jnp.zeros((t, 1), F32)
    carry = tile(qi, (zero, zero), causal)
    lax.fori_loop(0, qi, lambda it, cr: tile(qi - 1 - it, cr, None), carry)
    o_ref[...] = acc_ref[...].astype(o_ref.dtype)


def _sb_attention(q, k, v, *, t):
    b, s, _ = q.shape
    pairs = SB_HEADS // 2
    qspec = pl.BlockSpec((None, t, LANES), lambda bi, pi, qi: (bi, qi, pi))
    kvspec = pl.BlockSpec((None, s, LANES), lambda bi, pi, qi: (bi, 0, pi))
    return pl.pallas_call(
        functools.partial(_sb_kernel, t=t),
        out_shape=jax.ShapeDtypeStruct((b, s, SB_W), BF16),
        grid=(b, pairs, s // t),
        in_specs=[qspec, kvspec, kvspec],
        out_specs=qspec,
        scratch_shapes=[pltpu.VMEM((t, LANES), F32)],
        compiler_params=_params(("parallel", "parallel", "arbitrary")),
        name="stick_breaking_attention",
    )(q, k, v)


def _outproj_kernel(x_ref, a_ref, b_ref, c_ref, w_ref, o_ref):
    wa = DIFF_W
    wb = wa + MLA_V_W
    o_ref[...] = (x_ref[...] + _dot(a_ref[...], w_ref[0:wa, :]) + _dot(b_ref[...], w_ref[wa:wb, :])
                  + _dot(c_ref[...], w_ref[wb:wb + SB_W, :]))


def _outproj(h, oa, ob, oc, w, *, tm):
    n, d = h.shape
    row = lambda wd: pl.BlockSpec((tm, wd), lambda i: (i, 0))
    return pl.pallas_call(
        _outproj_kernel,
        out_shape=jax.ShapeDtypeStruct((n, d), F32),
        grid=(n // tm,),
        in_specs=[row(d), row(DIFF_W), row(MLA_V_W), row(SB_W), _const_spec(w.shape)],
        out_specs=row(d),
        compiler_params=_params(("parallel",)),
        name="mixer_out_proj",
    )(h, oa, ob, oc, w)


def _rope_tables(rot_dim, seq, lane_starts):
    half = rot_dim // 2
    inv = ROPE_THETA ** (-jnp.arange(0, rot_dim, 2, dtype=F32) / rot_dim)
    ang = jnp.arange(seq, dtype=F32)[:, None] * inv[None, :]
    cos, sin = jnp.cos(ang), jnp.sin(ang)
    tc = jnp.ones((seq, LANES), F32)
    tu = jnp.zeros((seq, LANES), F32)
    td = jnp.zeros((seq, LANES), F32)
    for st in lane_starts:
        tc = tc.at[:, st:st + half].set(cos).at[:, st + half:st + 2 * half].set(cos)
        tu = tu.at[:, st:st + half].set(-sin)
        td = td.at[:, st + half:st + 2 * half].set(sin)
    return tc, tu, td


def _pack_in_weights(w_in, w_uq, w_ukv):
    d = w_in.shape[0]
    c_rope = DIFF_COLS + MLA_Q_LORA + MLA_KV_LORA
    rope_slab = jnp.zeros((d, LANES), F32).at[:, MLA_NOPE_DIM:MLA_NOPE_DIM + MLA_ROPE_DIM].set(
        w_in[:, c_rope:c_rope + MLA_ROPE_DIM])
    w = jnp.concatenate([w_in[:, :c_rope], rope_slab, w_in[:, c_rope + MLA_ROPE_DIM:]], axis=1)
    qd = MLA_NOPE_DIM + MLA_ROPE_DIM
    uq = jnp.pad(w_uq.reshape(MLA_Q_LORA, MLA_HEADS, qd), ((0, 0), (0, 0), (0, LANES - qd)))
    ukv = w_ukv.reshape(MLA_KV_LORA, MLA_HEADS, MLA_NOPE_DIM + MLA_V_DIM)
    uk = jnp.pad(ukv[:, :, :MLA_NOPE_DIM], ((0, 0), (0, 0), (0, LANES - MLA_NOPE_DIM)))
    uv = ukv[:, :, MLA_NOPE_DIM:]
    ukv_packed = jnp.concatenate([uk.reshape(MLA_KV_LORA, MLA_QK_W), uv.reshape(MLA_KV_LORA, MLA_V_W)], axis=1)
    return w.astype(BF16), uq.reshape(MLA_Q_LORA, MLA_QK_W).astype(BF16), ukv_packed.astype(BF16)


def kernel(x, ffn1_norm, ffn1_w_gate, ffn1_w_up, ffn1_w_down, mix_norm, w_in, diff_lambda_q1, diff_lambda_k1, diff_lambda_q2, diff_lambda_k2, diff_subln, mla_q_norm, mla_w_uq, mla_kv_norm, mla_w_ukv, w_out, ffn2_norm, ffn2_w_gate, ffn2_w_up, ffn2_w_down, final_norm):
    b, s, d = x.shape
    depth = w_in.shape[0]
    tm = min(ROW_TILE, s)
    t_soft = min(SOFTMAX_TILE, s)
    t_sb = min(SB_TILE, s)
    diff_tabs = _rope_tables(DIFF_ROT_DIM, s, (0, DIFF_QK_DIM))
    mla_tabs = _rope_tables(MLA_ROPE_DIM, s, (MLA_NOPE_DIM,))
    fn = final_norm.reshape(1, d)
    h = x.reshape(b * s, d)
    for i in range(depth):
        h = _ffn(h, ffn1_norm[i].reshape(1, d), ffn1_w_gate[i].astype(BF16), ffn1_w_up[i].astype(BF16),
                 ffn1_w_down[i].astype(BF16), fn, tm=tm, final_norm=False)
        w, uq, ukv = _pack_in_weights(w_in[i], mla_w_uq[i], mla_w_ukv[i])
        dq, dk, dv, mq, mk, mv, sq, sk, sv = _inproj(
            h, mix_norm[i].reshape(1, d), w, mla_q_norm[i].reshape(1, -1), mla_kv_norm[i].reshape(1, -1),
            uq, ukv, diff_tabs, mla_tabs, tm=tm, seq=s)
        shape3 = lambda a: a.reshape(b, s, a.shape[-1])
        lam_p = jnp.stack([diff_lambda_q1[i], diff_lambda_k1[i], diff_lambda_q2[i], diff_lambda_k2[i]])
        lambda_init = 0.8 - 0.6 * math.exp(-0.3 * i)
        oa = _diff_attention(lam_p, diff_subln[i].reshape(1, -1), shape3(dq), shape3(dk), shape3(dv),
                             t=t_soft, lambda_init=lambda_init)
        ob = _mla_attention(shape3(mq), shape3(mk), shape3(mv), t=t_soft)
        oc = _sb_attention(shape3(sq), shape3(sk), shape3(sv), t=t_sb)
        flat = lambda a: a.reshape(b * s, a.shape[-1])
        h = _outproj(h, flat(oa), flat(ob), flat(oc), w_out[i].astype(BF16), tm=tm)
        h = _ffn(h, ffn2_norm[i].reshape(1, d), ffn2_w_gate[i].astype(BF16), ffn2_w_up[i].astype(BF16),
                 ffn2_w_down[i].astype(BF16), fn, tm=tm, final_norm=(i == depth - 1))
    return h.reshape(b, s, d)
```

```python
import functools
import math

import jax
import jax.numpy as jnp
from jax import lax
from jax.experimental import pallas as pl
from jax.experimental.pallas import tpu as pltpu

F32 = jnp.float32
BF16 = jnp.bfloat16

D_MODEL = 1024
CHUNK = 64
RMS_EPS = 1e-6
ROPE_THETA = 500000.0

DIFF_HEADS = 4
DIFF_QK_DIM = 64
DIFF_V_DIM = 128
DIFF_ROT_DIM = 16

MLA_HEADS = 4
MLA_Q_LORA = 256
MLA_KV_LORA = 128
MLA_NOPE_DIM = 64
MLA_ROPE_DIM = 32
MLA_V_DIM = 64

SB_HEADS = 4
SB_DIM = 64

LANES = 128
DIFF_W = DIFF_HEADS * LANES
MLA_QK_W = MLA_HEADS * LANES
MLA_V_W = MLA_HEADS * MLA_V_DIM
SB_W = SB_HEADS * SB_DIM
DIFF_COLS = 3 * DIFF_W
IN_COLS_PADDED = DIFF_COLS + MLA_Q_LORA + MLA_KV_LORA + LANES + 3 * SB_W

VMEM_LIMIT_BYTES = 56 * 1024 * 1024

ROW_TILE = 512
SOFTMAX_TILE = 512
SB_TILE = 256
SB_DEAD_CARRY = 104.0


def _rms(x, g):
    return x * lax.rsqrt(jnp.mean(x * x, axis=-1, keepdims=True) + RMS_EPS) * g


def _dot(a, b):
    return jnp.dot(a, b, preferred_element_type=F32)


def _dot_nt(a, b):
    return lax.dot_general(a, b, (((1,), (1,)), ((), ())), preferred_element_type=F32)


def _const_spec(shape):
    return pl.BlockSpec(shape, lambda *_: (0,) * len(shape), pipeline_mode=pl.Buffered(1))


def _params(semantics):
    return pltpu.CompilerParams(dimension_semantics=semantics, vmem_limit_bytes=VMEM_LIMIT_BYTES)


def _ffn_kernel(x_ref, g_ref, wg_ref, wu_ref, wd_ref, fn_ref, o_ref, *, ff_chunk, final_norm):
    x = x_ref[...]
    hn = _rms(x, g_ref[...]).astype(BF16)
    d_ff = wg_ref.shape[1]
    acc = jnp.zeros(x.shape, F32)
    for c in range(d_ff // ff_chunk):
        sl = slice(c * ff_chunk, (c + 1) * ff_chunk)
        gate = _dot(hn, wg_ref[:, sl])
        up = _dot(hn, wu_ref[:, sl])
        act = (gate / (1.0 + jnp.exp(-gate)) * up).astype(BF16)
        acc = acc + _dot(act, wd_ref[sl, :])
    y = x + 0.5 * acc
    if final_norm:
        y = _rms(y, fn_ref[...])
    o_ref[...] = y


def _ffn(h, g, wg, wu, wd, fn, *, tm, final_norm):
    n, d = h.shape
    d_ff = wg.shape[1]
    ff_chunk = d_ff // 2 if (d_ff // 2) % LANES == 0 else d_ff
    row = pl.BlockSpec((tm, d), lambda i: (i, 0))
    return pl.pallas_call(
        functools.partial(_ffn_kernel, ff_chunk=ff_chunk, final_norm=final_norm),
        out_shape=jax.ShapeDtypeStruct((n, d), F32),
        grid=(n // tm,),
        in_specs=[row, _const_spec((1, d)), _const_spec((d, d_ff)), _const_spec((d, d_ff)),
                  _const_spec((d_ff, d)), _const_spec((1, d))],
        out_specs=row,
        compiler_params=_params(("parallel",)),
        name="swiglu_half_step",
    )(h, g, wg, wu, wd, fn)


def _rope(x, cos, sin_up, sin_down, half):
    n = x.shape[-1]
    return x * cos + pltpu.roll(x, n - half, 1) * sin_up + pltpu.roll(x, half, 1) * sin_down


def _inproj_kernel(x_ref, g_ref, w_ref, qn_ref, kvn_ref, wuq_ref, wukv_ref,
                   dc_ref, du_ref, dd_ref, mc_ref, mu_ref, md_ref,
                   dq_ref, dk_ref, dv_ref, mq_ref, mk_ref, mv_ref, sq_ref, sk_ref, sv_ref):
    hn = _rms(x_ref[...], g_ref[...]).astype(BF16)
    dc, du, dd = dc_ref[...], du_ref[...], dd_ref[...]
    mc, mu, md = mc_ref[...], mu_ref[...], md_ref[...]
    d_half = DIFF_ROT_DIM // 2
    m_half = MLA_ROPE_DIM // 2

    q_scale = DIFF_QK_DIM ** -0.5
    for hd in range(DIFF_HEADS):
        sl = slice(hd * LANES, (hd + 1) * LANES)
        q = _dot(hn, w_ref[:, hd * LANES:(hd + 1) * LANES])
        dq_ref[:, sl] = (_rope(q, dc, du, dd, d_half) * q_scale).astype(BF16)
        k = _dot(hn, w_ref[:, DIFF_W + hd * LANES:DIFF_W + (hd + 1) * LANES])
        dk_ref[:, sl] = _rope(k, dc, du, dd, d_half).astype(BF16)
    dv_ref[...] = _dot(hn, w_ref[:, 2 * DIFF_W:3 * DIFF_W]).astype(BF16)

    mla_scale = (MLA_NOPE_DIM + MLA_ROPE_DIM) ** -0.5
    c0 = DIFF_COLS
    c_q = _rms(_dot(hn, w_ref[:, c0:c0 + MLA_Q_LORA]), qn_ref[...]).astype(BF16)
    c0 += MLA_Q_LORA
    c_kv = _rms(_dot(hn, w_ref[:, c0:c0 + MLA_KV_LORA]), kvn_ref[...]).astype(BF16)
    c0 += MLA_KV_LORA
    k_rope = _rope(_dot(hn, w_ref[:, c0:c0 + LANES]), mc, mu, md, m_half)
    c0 += LANES
    for hd in range(MLA_HEADS):
        sl = slice(hd * LANES, (hd + 1) * LANES)
        q = _dot(c_q, wuq_ref[:, sl])
        mq_ref[:, sl] = (_rope(q, mc, mu, md, m_half) * mla_scale).astype(BF16)
        k_nope = _dot(c_kv, wukv_ref[:, sl])
        mk_ref[:, sl] = (k_nope + k_rope).astype(BF16)
    mv_ref[...] = _dot(c_kv, wukv_ref[:, MLA_QK_W:MLA_QK_W + MLA_V_W]).astype(BF16)

    sq_ref[...] = (_dot(hn, w_ref[:, c0:c0 + SB_W]) * (SB_DIM ** -0.5)).astype(BF16)
    sk_ref[...] = _dot(hn, w_ref[:, c0 + SB_W:c0 + 2 * SB_W]).astype(BF16)
    sv_ref[...] = _dot(hn, w_ref[:, c0 + 2 * SB_W:c0 + 3 * SB_W]).astype(BF16)


def _inproj(h, g, w, qn, kvn, wuq, wukv, diff_tabs, mla_tabs, *, tm, seq):
    n, d = h.shape
    pos_blocks = seq // tm
    row = pl.BlockSpec((tm, d), lambda i: (i, 0))
    tab = pl.BlockSpec((tm, LANES), lambda i: (i % pos_blocks, 0))
    widths = (DIFF_W, DIFF_W, DIFF_W, MLA_QK_W, MLA_QK_W, MLA_V_W, SB_W, SB_W, SB_W)
    return pl.pallas_call(
        _inproj_kernel,
        out_shape=tuple(jax.ShapeDtypeStruct((n, wd), BF16) for wd in widths),
        grid=(n // tm,),
        in_specs=[row, _const_spec((1, d)), _const_spec(w.shape), _const_spec(qn.shape),
                  _const_spec(kvn.shape), _const_spec(wuq.shape), _const_spec(wukv.shape)] + [tab] * 6,
        out_specs=tuple(pl.BlockSpec((tm, wd), lambda i: (i, 0)) for wd in widths),
        compiler_params=_params(("parallel",)),
        name="mixer_in_proj",
    )(h, g, w, qn, kvn, wuq, wukv, *diff_tabs, *mla_tabs)


def _chunk_mask(tq, tk, row_offset):
    row = lax.broadcasted_iota(jnp.int32, (tq, tk), 0) + row_offset
    col = lax.broadcasted_iota(jnp.int32, (tq, tk), 1)
    return (col // CHUNK) <= (row // CHUNK)


def _softmax_update(s, m, l, acc_ref, v):
    m_new = jnp.maximum(m, jnp.max(s, axis=-1, keepdims=True))
    alpha = jnp.exp(m - m_new)
    p = jnp.exp(s - m_new)
    l_new = alpha * l + jnp.sum(p, axis=-1, keepdims=True)
    acc_ref[...] = alpha * acc_ref[...] + _dot(p.astype(BF16), v)
    return m_new, l_new


def _diff_kernel(lam_ref, subln_ref, q_ref, k_ref, v_ref, o_ref, acc1_ref, acc2_ref, *, t, lambda_init):
    qi = pl.program_id(2)
    q = q_ref[...]
    lane = lax.broadcasted_iota(jnp.int32, q.shape, 1)
    q1 = jnp.where(lane < DIFF_QK_DIM, q, jnp.zeros_like(q))
    q2 = jnp.where(lane >= DIFF_QK_DIM, q, jnp.zeros_like(q))
    acc1_ref[...] = jnp.zeros_like(acc1_ref)
    acc2_ref[...] = jnp.zeros_like(acc2_ref)

    def tile(kj, carry, mask):
        m1, l1, m2, l2 = carry
        start = pl.multiple_of(kj * t, t)
        k = k_ref[pl.ds(start, t), :]
        v = v_ref[pl.ds(start, t), :]
        s1 = _dot_nt(q1, k)
        s2 = _dot_nt(q2, k)
        if mask is not None:
            s1 = jnp.where(mask, s1, -jnp.inf)
            s2 = jnp.where(mask, s2, -jnp.inf)
        m1, l1 = _softmax_update(s1, m1, l1, acc1_ref, v)
        m2, l2 = _softmax_update(s2, m2, l2, acc2_ref, v)
        return m1, l1, m2, l2

    neg = jnp.full((t, 1), -jnp.inf, F32)
    zero = jnp.zeros((t, 1), F32)
    carry = lax.fori_loop(0, qi, lambda kj, c: tile(kj, c, None), (neg, zero, neg, zero))
    m1, l1, m2, l2 = tile(qi, carry, _chunk_mask(t, t, 0))

    lam_p = lam_ref[...]
    lam = (jnp.exp(jnp.sum(lam_p[0:1] * lam_p[1:2], axis=-1, keepdims=True))
           - jnp.exp(jnp.sum(lam_p[2:3] * lam_p[3:4], axis=-1, keepdims=True)) + lambda_init)
    o = acc1_ref[...] / l1 - lam * (acc2_ref[...] / l2)
    o_ref[...] = (_rms(o, subln_ref[...]) * (1.0 - lambda_init)).astype(o_ref.dtype)


def _diff_attention(lam_p, subln, q, k, v, *, t, lambda_init):
    b, s, _ = q.shape
    qspec = pl.BlockSpec((None, t, LANES), lambda bi, hi, qi: (bi, qi, hi))
    kvspec = pl.BlockSpec((None, s, LANES), lambda bi, hi, qi: (bi, 0, hi))
    return pl.pallas_call(
        functools.partial(_diff_kernel, t=t, lambda_init=lambda_init),
        out_shape=jax.ShapeDtypeStruct((b, s, DIFF_W), BF16),
        grid=(b, DIFF_HEADS, s // t),
        in_specs=[_const_spec(lam_p.shape), _const_spec(subln.shape), qspec, kvspec, kvspec],
        out_specs=qspec,
        scratch_shapes=[pltpu.VMEM((t, LANES), F32)] * 2,
        compiler_params=_params(("parallel", "parallel", "arbitrary")),
        name="diff_attention",
    )(lam_p, subln, q, k, v)


def _mla_kernel(q_ref, k_ref, v_ref, o_ref, acc_ref, *, t):
    qi = pl.program_id(2)
    lane = lax.broadcasted_iota(jnp.int32, (t, LANES), 1)
    first = lane < MLA_V_DIM
    q0 = q_ref[:, 0:LANES]
    q1 = q_ref[:, LANES:2 * LANES]

    def probs(s, m, l):
        m_new = jnp.maximum(m, jnp.max(s, axis=-1, keepdims=True))
        alpha = jnp.exp(m - m_new)
        p = jnp.exp(s - m_new)
        return p.astype(BF16), alpha, m_new, alpha * l + jnp.sum(p, axis=-1, keepdims=True)

    def tile(kj, carry, msk):
        m0, l0, m1, l1 = carry
        start = pl.multiple_of(kj * t, t)
        s0 = _dot_nt(q0, k_ref[pl.ds(start, t), 0:LANES])
        s1 = _dot_nt(q1, k_ref[pl.ds(start, t), LANES:2 * LANES])
        if msk is not None:
            s0 = jnp.where(msk, s0, -jnp.inf)
            s1 = jnp.where(msk, s1, -jnp.inf)
        p0, alpha0, m0, l0 = probs(s0, m0, l0)
        p1, alpha1, m1, l1 = probs(s1, m1, l1)
        v = v_ref[pl.ds(start, t), :]
        vz = jnp.zeros_like(v)
        pv = _dot(p0, jnp.where(first, v, vz)) + _dot(p1, jnp.where(first, vz, v))
        acc_ref[...] = jnp.where(first, alpha0, alpha1) * acc_ref[...] + pv
        return m0, l0, m1, l1

    acc_ref[...] = jnp.zeros_like(acc_ref)
    neg = jnp.full((t, 1), -jnp.inf, F32)
    zero = jnp.zeros((t, 1), F32)
    carry = lax.fori_loop(0, qi, lambda kj, cr: tile(kj, cr, None), (neg, zero, neg, zero))
    _, l0, _, l1 = tile(qi, carry, _chunk_mask(t, t, 0))
    o_ref[...] = (acc_ref[...] / jnp.where(first, l0, l1)).astype(o_ref.dtype)


def _mla_attention(q, k, v, *, t):
    b, s, _ = q.shape
    pairs = MLA_HEADS // 2
    return pl.pallas_call(
        functools.partial(_mla_kernel, t=t),
        out_shape=jax.ShapeDtypeStruct((b, s, MLA_V_W), BF16),
        grid=(b, pairs, s // t),
        in_specs=[pl.BlockSpec((None, t, 2 * LANES), lambda bi, pi, qi: (bi, qi, pi)),
                  pl.BlockSpec((None, s, 2 * LANES), lambda bi, pi, qi: (bi, 0, pi)),
                  pl.BlockSpec((None, s, LANES), lambda bi, pi, qi: (bi, 0, pi))],
        out_specs=pl.BlockSpec((None, t, LANES), lambda bi, pi, qi: (bi, qi, pi)),
        scratch_shapes=[pltpu.VMEM((t, LANES), F32)],
        compiler_params=_params(("parallel", "parallel", "arbitrary")),
        name="mla_attention",
    )(q, k, v)


def _sb_kernel(q_ref, k_ref, v_ref, o_ref, acc_ref, *, t):
    qi = pl.program_id(2)
    row = lax.broadcasted_iota(jnp.int32, (t, t), 0)
    col = lax.broadcasted_iota(jnp.int32, (t, t), 1)
    causal = col < row
    after_mat = jnp.where(row > col, 1.0, 0.0).astype(BF16)
    q_all = q_ref[...]
    lane = lax.broadcasted_iota(jnp.int32, q_all.shape, 1)
    first = lane < SB_DIM
    zeros = jnp.zeros_like(q_all)
    qs = (jnp.where(first, q_all, zeros), jnp.where(first, zeros, q_all))
    sign_bit = jnp.uint32(0x80000000)

    def weights(q, k, msk):
        z = _dot_nt(q, k)
        neg_abs = lax.bitcast_convert_type(lax.bitcast_convert_type(z, jnp.uint32) | sign_bit, F32)
        softplus = jnp.maximum(z, 0.0) + jnp.log(1.0 + jnp.exp(neg_abs))
        log_sig = z - softplus
        if msk is not None:
            softplus = jnp.where(msk, softplus, 0.0)
        a = jnp.exp(log_sig - _dot(softplus.astype(BF16), after_mat))
        if msk is not None:
            a = jnp.where(msk, a, 0.0)
        return a.astype(BF16), jnp.sum(softplus, axis=-1, keepdims=True)

    def tile(kj, carry, msk):
        c0, c1 = carry
        start = pl.multiple_of(kj * t, t)
        k = k_ref[pl.ds(start, t), :]
        v = v_ref[pl.ds(start, t), :]
        vz = jnp.zeros_like(v)
        a0, r0 = weights(qs[0], k, msk)
        a1, r1 = weights(qs[1], k, msk)
        pv = _dot(a0, jnp.where(first, v, vz)) + _dot(a1, jnp.where(first, vz, v))
        acc_ref[...] += pv * jnp.where(first, jnp.exp(-c0), jnp.exp(-c1))
        return c0 + r0, c1 + r1

    def live(c0, c1):
        return (jnp.min(jnp.minimum(c0, c1)) < SB_DEAD_CARRY).astype(jnp.int32)

    def step(state):
        it, _, c0, c1 = state
        c0, c1 = tile(qi - 1 - it, (c0, c1), None)
        return it + 1, live(c0, c1), c0, c1

    acc_ref[...] = jnp.zeros_like(acc_ref)
    zero = jnp.zeros((t, 1), F32)
    c0, c1 = tile(qi, (zero, zero), causal)
    lax.while_loop(lambda st: (st[0] < qi) & (st[1] > 0), step, (jnp.int32(0), live(c0, c1), c0, c1))
    o_ref[...] = acc_ref[...].astype(o_ref.dtype)


def _sb_attention(q, k, v, *, t):
    b, s, _ = q.shape
    pairs = SB_HEADS // 2
    qspec = pl.BlockSpec((None, t, LANES), lambda bi, pi, qi: (bi, qi, pi))
    kvspec = pl.BlockSpec((None, s, LANES), lambda bi, pi, qi: (bi, 0, pi))
    return pl.pallas_call(
        functools.partial(_sb_kernel, t=t),
        out_shape=jax.ShapeDtypeStruct((b, s, SB_W), BF16),
        grid=(b, pairs, s // t),
        in_specs=[qspec, kvspec, kvspec],
        out_specs=qspec,
        scratch_shapes=[pltpu.VMEM((t, LANES), F32)],
        compiler_params=_params(("parallel", "parallel", "arbitrary")),
        name="stick_breaking_attention",
    )(q, k, v)


def _outproj_kernel(x_ref, a_ref, b_ref, c_ref, w_ref, o_ref):
    wa = DIFF_W
    wb = wa + MLA_V_W
    o_ref[...] = (x_ref[...] + _dot(a_ref[...], w_ref[0:wa, :]) + _dot(b_ref[...], w_ref[wa:wb, :])
                  + _dot(c_ref[...], w_ref[wb:wb + SB_W, :]))


def _outproj(h, oa, ob, oc, w, *, tm):
    n, d = h.shape
    row = lambda wd: pl.BlockSpec((tm, wd), lambda i: (i, 0))
    return pl.pallas_call(
        _outproj_kernel,
        out_shape=jax.ShapeDtypeStruct((n, d), F32),
        grid=(n // tm,),
        in_specs=[row(d), row(DIFF_W), row(MLA_V_W), row(SB_W), _const_spec(w.shape)],
        out_specs=row(d),
        compiler_params=_params(("parallel",)),
        name="mixer_out_proj",
    )(h, oa, ob, oc, w)


def _rope_tables(rot_dim, seq, lane_starts):
    half = rot_dim // 2
    inv = ROPE_THETA ** (-jnp.arange(0, rot_dim, 2, dtype=F32) / rot_dim)
    ang = jnp.arange(seq, dtype=F32)[:, None] * inv[None, :]
    cos, sin = jnp.cos(ang), jnp.sin(ang)
    tc = jnp.ones((seq, LANES), F32)
    tu = jnp.zeros((seq, LANES), F32)
    td = jnp.zeros((seq, LANES), F32)
    for st in lane_starts:
        tc = tc.at[:, st:st + half].set(cos).at[:, st + half:st + 2 * half].set(cos)
        tu = tu.at[:, st:st + half].set(-sin)
        td = td.at[:, st + half:st + 2 * half].set(sin)
    return tc, tu, td


def _pack_in_weights(w_in, w_uq, w_ukv):
    d = w_in.shape[0]
    c_rope = DIFF_COLS + MLA_Q_LORA + MLA_KV_LORA
    rope_slab = jnp.zeros((d, LANES), F32).at[:, MLA_NOPE_DIM:MLA_NOPE_DIM + MLA_ROPE_DIM].set(
        w_in[:, c_rope:c_rope + MLA_ROPE_DIM])
    w = jnp.concatenate([w_in[:, :c_rope], rope_slab, w_in[:, c_rope + MLA_ROPE_DIM:]], axis=1)
    qd = MLA_NOPE_DIM + MLA_ROPE_DIM
    uq = jnp.pad(w_uq.reshape(MLA_Q_LORA, MLA_HEADS, qd), ((0, 0), (0, 0), (0, LANES - qd)))
    ukv = w_ukv.reshape(MLA_KV_LORA, MLA_HEADS, MLA_NOPE_DIM + MLA_V_DIM)
    uk = jnp.pad(ukv[:, :, :MLA_NOPE_DIM], ((0, 0), (0, 0), (0, LANES - MLA_NOPE_DIM)))
    uv = ukv[:, :, MLA_NOPE_DIM:]
    ukv_packed = jnp.concatenate([uk.reshape(MLA_KV_LORA, MLA_QK_W), uv.reshape(MLA_KV_LORA, MLA_V_W)], axis=1)
    return w.astype(BF16), uq.reshape(MLA_Q_LORA, MLA_QK_W).astype(BF16), ukv_packed.astype(BF16)


def kernel(x, ffn1_norm, ffn1_w_gate, ffn1_w_up, ffn1_w_down, mix_norm, w_in, diff_lambda_q1, diff_lambda_k1, diff_lambda_q2, diff_lambda_k2, diff_subln, mla_q_norm, mla_w_uq, mla_kv_norm, mla_w_ukv, w_out, ffn2_norm, ffn2_w_gate, ffn2_w_up, ffn2_w_down, final_norm):
    b, s, d = x.shape
    depth = w_in.shape[0]
    tm = min(ROW_TILE, s)
    t_soft = min(SOFTMAX_TILE, s)
    t_sb = min(SB_TILE, s)
    diff_tabs = _rope_tables(DIFF_ROT_DIM, s, (0, DIFF_QK_DIM))
    mla_tabs = _rope_tables(MLA_ROPE_DIM, s, (MLA_NOPE_DIM,))
    fn = final_norm.reshape(1, d)
    h = x.reshape(b * s, d)
    for i in range(depth):
        h = _ffn(h, ffn1_norm[i].reshape(1, d), ffn1_w_gate[i].astype(BF16), ffn1_w_up[i].astype(BF16),
                 ffn1_w_down[i].astype(BF16), fn, tm=tm, final_norm=False)
        w, uq, ukv = _pack_in_weights(w_in[i], mla_w_uq[i], mla_w_ukv[i])
        dq, dk, dv, mq, mk, mv, sq, sk, sv = _inproj(
            h, mix_norm[i].reshape(1, d), w, mla_q_norm[i].reshape(1, -1), mla_kv_norm[i].reshape(1, -1),
            uq, ukv, diff_tabs, mla_tabs, tm=tm, seq=s)
        shape3 = lambda a: a.reshape(b, s, a.shape[-1])
        lam_p = jnp.stack([diff_lambda_q1[i], diff_lambda_k1[i], diff_lambda_q2[i], diff_lambda_k2[i]])
        lambda_init = 0.8 - 0.6 * math.exp(-0.3 * i)
        oa = _diff_attention(lam_p, diff_subln[i].reshape(1, -1), shape3(dq), shape3(dk), shape3(dv),
                             t=t_soft, lambda_init=lambda_init)
        ob = _mla_attention(shape3(mq), shape3(mk), shape3(mv), t=t_soft)
        oc = _sb_attention(shape3(sq), shape3(sk), shape3(sv), t=t_sb)
        flat = lambda a: a.reshape(b * s, a.shape[-1])
        h = _outproj(h, flat(oa), flat(ob), flat(oc), w_out[i].astype(BF16), tm=tm)
        h = _ffn(h, ffn2_norm[i].reshape(1, d), ffn2_w_gate[i].astype(BF16), ffn2_w_up[i].astype(BF16),
                 ffn2_w_down[i].astype(BF16), fn, tm=tm, final_norm=(i == depth - 1))
    return h.reshape(b, s, d)
```

```python
import functools
import math

import jax
import jax.numpy as jnp
from jax import lax
from jax.experimental import pallas as pl
from jax.experimental.pallas import tpu as pltpu

F32 = jnp.float32
BF16 = jnp.bfloat16

D_MODEL = 1024
CHUNK = 64
RMS_EPS = 1e-6
ROPE_THETA = 500000.0

DIFF_HEADS = 4
DIFF_QK_DIM = 64
DIFF_V_DIM = 128
DIFF_ROT_DIM = 16

MLA_HEADS = 4
MLA_Q_LORA = 256
MLA_KV_LORA = 128
MLA_NOPE_DIM = 64
MLA_ROPE_DIM = 32
MLA_V_DIM = 64

SB_HEADS = 4
SB_DIM = 64

LANES = 128
MXU_COLS = 256
DIFF_W = DIFF_HEADS * LANES
MLA_QK_W = MLA_HEADS * LANES
MLA_V_W = MLA_HEADS * MLA_V_DIM
SB_W = SB_HEADS * SB_DIM
DIFF_COLS = 3 * DIFF_W
IN_COLS_PADDED = DIFF_COLS + MLA_Q_LORA + MLA_KV_LORA + LANES + 3 * SB_W

VMEM_LIMIT_BYTES = 56 * 1024 * 1024

ROW_TILE = 512
SOFTMAX_TILE = 1024
SB_TILE = 256
SB_DEAD_CARRY = 104.0


def _rms(x, g):
    return x * lax.rsqrt(jnp.mean(x * x, axis=-1, keepdims=True) + RMS_EPS) * g


def _dot(a, b):
    return jnp.dot(a, b, preferred_element_type=F32)


def _dot_nt(a, b):
    return lax.dot_general(a, b, (((1,), (1,)), ((), ())), preferred_element_type=F32)


def _const_spec(shape):
    return pl.BlockSpec(shape, lambda *_: (0,) * len(shape), pipeline_mode=pl.Buffered(1))


def _params(semantics):
    return pltpu.CompilerParams(dimension_semantics=semantics, vmem_limit_bytes=VMEM_LIMIT_BYTES)


def _ffn_kernel(x_ref, g_ref, wg_ref, wu_ref, wd_ref, fn_ref, o_ref, *, ff_chunks, final_norm):
    x = x_ref[...]
    hn = _rms(x, g_ref[...]).astype(BF16)
    acc = jnp.zeros(x.shape, F32)
    for c in range(len(ff_chunks) - 1):
        sl = slice(ff_chunks[c], ff_chunks[c + 1])
        gate = _dot(hn, wg_ref[:, sl])
        up = _dot(hn, wu_ref[:, sl])
        act = (gate / (1.0 + jnp.exp(-gate)) * up).astype(BF16)
        acc = acc + _dot(act, wd_ref[sl, :])
    y = x + 0.5 * acc
    if final_norm:
        y = _rms(y, fn_ref[...])
    o_ref[...] = y


def _ffn(h, g, wg, wu, wd, fn, *, tm, final_norm):
    n, d = h.shape
    d_ff = wg.shape[1]
    mid = (d_ff // 2 + MXU_COLS - 1) // MXU_COLS * MXU_COLS
    ff_chunks = (0, mid, d_ff) if 0 < mid < d_ff else (0, d_ff)
    row = pl.BlockSpec((tm, d), lambda i: (i, 0))
    return pl.pallas_call(
        functools.partial(_ffn_kernel, ff_chunks=ff_chunks, final_norm=final_norm),
        out_shape=jax.ShapeDtypeStruct((n, d), F32),
        grid=(n // tm,),
        in_specs=[row, _const_spec((1, d)), _const_spec((d, d_ff)), _const_spec((d, d_ff)),
                  _const_spec((d_ff, d)), _const_spec((1, d))],
        out_specs=row,
        compiler_params=_params(("parallel",)),
        name="swiglu_half_step",
    )(h, g, wg, wu, wd, fn)


def _rope(x, cos, sin_up, sin_down, half):
    n = x.shape[-1]
    return x * cos + pltpu.roll(x, n - half, 1) * sin_up + pltpu.roll(x, half, 1) * sin_down


def _inproj_kernel(x_ref, g_ref, w_ref, qn_ref, kvn_ref, wuq_ref, wukv_ref,
                   dc_ref, du_ref, dd_ref, mc_ref, mu_ref, md_ref,
                   dq_ref, dk_ref, dv_ref, mq_ref, mk_ref, mv_ref, sq_ref, sk_ref, sv_ref):
    hn = _rms(x_ref[...], g_ref[...]).astype(BF16)
    dc, du, dd = dc_ref[...], du_ref[...], dd_ref[...]
    mc, mu, md = mc_ref[...], mu_ref[...], md_ref[...]
    d_half = DIFF_ROT_DIM // 2
    m_half = MLA_ROPE_DIM // 2

    q_scale = DIFF_QK_DIM ** -0.5
    q = _dot(hn, w_ref[:, 0:DIFF_W])
    k = _dot(hn, w_ref[:, DIFF_W:2 * DIFF_W])
    for hd in range(DIFF_HEADS):
        sl = slice(hd * LANES, (hd + 1) * LANES)
        dq_ref[:, sl] = (_rope(q[:, sl], dc, du, dd, d_half) * q_scale).astype(BF16)
        dk_ref[:, sl] = _rope(k[:, sl], dc, du, dd, d_half).astype(BF16)
    dv_ref[...] = _dot(hn, w_ref[:, 2 * DIFF_W:3 * DIFF_W]).astype(BF16)

    mla_scale = (MLA_NOPE_DIM + MLA_ROPE_DIM) ** -0.5
    c0 = DIFF_COLS
    c_q = _rms(_dot(hn, w_ref[:, c0:c0 + MLA_Q_LORA]), qn_ref[...]).astype(BF16)
    c0 += MLA_Q_LORA
    kv = _dot(hn, w_ref[:, c0:c0 + MLA_KV_LORA + LANES])
    c_kv = _rms(kv[:, 0:MLA_KV_LORA], kvn_ref[...]).astype(BF16)
    k_rope = _rope(kv[:, MLA_KV_LORA:], mc, mu, md, m_half)
    c0 += MLA_KV_LORA + LANES
    q = _dot(c_q, wuq_ref[...])
    up = _dot(c_kv, wukv_ref[...])
    for hd in range(MLA_HEADS):
        sl = slice(hd * LANES, (hd + 1) * LANES)
        mq_ref[:, sl] = (_rope(q[:, sl], mc, mu, md, m_half) * mla_scale).astype(BF16)
        mk_ref[:, sl] = (up[:, sl] + k_rope).astype(BF16)
    mv_ref[...] = up[:, MLA_QK_W:].astype(BF16)

    sq_ref[...] = (_dot(hn, w_ref[:, c0:c0 + SB_W]) * (SB_DIM ** -0.5)).astype(BF16)
    sk_ref[...] = _dot(hn, w_ref[:, c0 + SB_W:c0 + 2 * SB_W]).astype(BF16)
    sv_ref[...] = _dot(hn, w_ref[:, c0 + 2 * SB_W:c0 + 3 * SB_W]).astype(BF16)


def _inproj(h, g, w, qn, kvn, wuq, wukv, diff_tabs, mla_tabs, *, tm, seq):
    n, d = h.shape
    pos_blocks = seq // tm
    row = pl.BlockSpec((tm, d), lambda i: (i, 0))
    tab = pl.BlockSpec((tm, LANES), lambda i: (i % pos_blocks, 0))
    widths = (DIFF_W, DIFF_W, DIFF_W, MLA_QK_W, MLA_QK_W, MLA_V_W, SB_W, SB_W, SB_W)
    return pl.pallas_call(
        _inproj_kernel,
        out_shape=tuple(jax.ShapeDtypeStruct((n, wd), BF16) for wd in widths),
        grid=(n // tm,),
        in_specs=[row, _const_spec((1, d)), _const_spec(w.shape), _const_spec(qn.shape),
                  _const_spec(kvn.shape), _const_spec(wuq.shape), _const_spec(wukv.shape)] + [tab] * 6,
        out_specs=tuple(pl.BlockSpec((tm, wd), lambda i: (i, 0)) for wd in widths),
        compiler_params=_params(("parallel",)),
        name="mixer_in_proj",
    )(h, g, w, qn, kvn, wuq, wukv, *diff_tabs, *mla_tabs)


def _chunk_mask(tq, tk, row_offset):
    row = lax.broadcasted_iota(jnp.int32, (tq, tk), 0) + row_offset
    col = lax.broadcasted_iota(jnp.int32, (tq, tk), 1)
    return (col // CHUNK) <= (row // CHUNK)


def _softmax_update(s, m, l, acc_ref, v):
    m_new = jnp.maximum(m, jnp.max(s, axis=-1, keepdims=True))
    alpha = jnp.exp(m - m_new)
    p = jnp.exp(s - m_new)
    l_new = alpha * l + jnp.sum(p, axis=-1, keepdims=True)
    acc_ref[...] = alpha * acc_ref[...] + _dot(p.astype(BF16), v)
    return m_new, l_new


def _diff_kernel(lam_ref, subln_ref, q_ref, k_ref, v_ref, o_ref, acc1_ref, acc2_ref, *, t, lambda_init):
    qi = pl.program_id(2)
    q = q_ref[...]
    lane = lax.broadcasted_iota(jnp.int32, q.shape, 1)
    q1 = jnp.where(lane < DIFF_QK_DIM, q, jnp.zeros_like(q))
    q2 = jnp.where(lane >= DIFF_QK_DIM, q, jnp.zeros_like(q))
    acc1_ref[...] = jnp.zeros_like(acc1_ref)
    acc2_ref[...] = jnp.zeros_like(acc2_ref)

    def tile(kj, carry, mask):
        m1, l1, m2, l2 = carry
        start = pl.multiple_of(kj * t, t)
        k = k_ref[pl.ds(start, t), :]
        v = v_ref[pl.ds(start, t), :]
        s1 = _dot_nt(q1, k)
        s2 = _dot_nt(q2, k)
        if mask is not None:
            s1 = jnp.where(mask, s1, -jnp.inf)
            s2 = jnp.where(mask, s2, -jnp.inf)
        m1, l1 = _softmax_update(s1, m1, l1, acc1_ref, v)
        m2, l2 = _softmax_update(s2, m2, l2, acc2_ref, v)
        return m1, l1, m2, l2

    neg = jnp.full((t, 1), -jnp.inf, F32)
    zero = jnp.zeros((t, 1), F32)
    carry = lax.fori_loop(0, qi, lambda kj, c: tile(kj, c, None), (neg, zero, neg, zero))
    m1, l1, m2, l2 = tile(qi, carry, _chunk_mask(t, t, 0))

    lam_p = lam_ref[...]
    lam = (jnp.exp(jnp.sum(lam_p[0:1] * lam_p[1:2], axis=-1, keepdims=True))
           - jnp.exp(jnp.sum(lam_p[2:3] * lam_p[3:4], axis=-1, keepdims=True)) + lambda_init)
    o = acc1_ref[...] / l1 - lam * (acc2_ref[...] / l2)
    o_ref[...] = (_rms(o, subln_ref[...]) * (1.0 - lambda_init)).astype(o_ref.dtype)


def _diff_attention(lam_p, subln, q, k, v, *, t, lambda_init):
    b, s, _ = q.shape
    qspec = pl.BlockSpec((None, t, LANES), lambda bi, hi, qi: (bi, qi, hi))
    kvspec = pl.BlockSpec((None, s, LANES), lambda bi, hi, qi: (bi, 0, hi))
    return pl.pallas_call(
        functools.partial(_diff_kernel, t=t, lambda_init=lambda_init),
        out_shape=jax.ShapeDtypeStruct((b, s, DIFF_W), BF16),
        grid=(b, DIFF_HEADS, s // t),
        in_specs=[_const_spec(lam_p.shape), _const_spec(subln.shape), qspec, kvspec, kvspec],
        out_specs=qspec,
        scratch_shapes=[pltpu.VMEM((t, LANES), F32)] * 2,
        compiler_params=_params(("parallel", "parallel", "arbitrary")),
        name="diff_attention",
    )(lam_p, subln, q, k, v)


def _mla_kernel(q_ref, k_ref, v_ref, o_ref, acc_ref, *, t):
    qi = pl.program_id(2)
    lane = lax.broadcasted_iota(jnp.int32, (t, LANES), 1)
    first = lane < MLA_V_DIM
    q0 = q_ref[:, 0:LANES]
    q1 = q_ref[:, LANES:2 * LANES]

    def probs(s, m, l):
        m_new = jnp.maximum(m, jnp.max(s, axis=-1, keepdims=True))
        alpha = jnp.exp(m - m_new)
        p = jnp.exp(s - m_new)
        return p.astype(BF16), alpha, m_new, alpha * l + jnp.sum(p, axis=-1, keepdims=True)

    def tile(kj, carry, msk):
        m0, l0, m1, l1 = carry
        start = pl.multiple_of(kj * t, t)
        s0 = _dot_nt(q0, k_ref[pl.ds(start, t), 0:LANES])
        s1 = _dot_nt(q1, k_ref[pl.ds(start, t), LANES:2 * LANES])
        if msk is not None:
            s0 = jnp.where(msk, s0, -jnp.inf)
            s1 = jnp.where(msk, s1, -jnp.inf)
        p0, alpha0, m0, l0 = probs(s0, m0, l0)
        p1, alpha1, m1, l1 = probs(s1, m1, l1)
        v = v_ref[pl.ds(start, t), :]
        vz = jnp.zeros_like(v)
        pv = _dot(p0, jnp.where(first, v, vz)) + _dot(p1, jnp.where(first, vz, v))
        acc_ref[...] = jnp.where(first, alpha0, alpha1) * acc_ref[...] + pv
        return m0, l0, m1, l1

    acc_ref[...] = jnp.zeros_like(acc_ref)
    neg = jnp.full((t, 1), -jnp.inf, F32)
    zero = jnp.zeros((t, 1), F32)
    carry = lax.fori_loop(0, qi, lambda kj, cr: tile(kj, cr, None), (neg, zero, neg, zero))
    _, l0, _, l1 = tile(qi, carry, _chunk_mask(t, t, 0))
    o_ref[...] = (acc_ref[...] / jnp.where(first, l0, l1)).astype(o_ref.dtype)


def _mla_attention(q, k, v, *, t):
    b, s, _ = q.shape
    pairs = MLA_HEADS // 2
    return pl.pallas_call(
        functools.partial(_mla_kernel, t=t),
        out_shape=jax.ShapeDtypeStruct((b, s, MLA_V_W), BF16),
        grid=(b, pairs, s // t),
        in_specs=[pl.BlockSpec((None, t, 2 * LANES), lambda bi, pi, qi: (bi, qi, pi)),
                  pl.BlockSpec((None, s, 2 * LANES), lambda bi, pi, qi: (bi, 0, pi)),
                  pl.BlockSpec((None, s, LANES), lambda bi, pi, qi: (bi, 0, pi))],
        out_specs=pl.BlockSpec((None, t, LANES), lambda bi, pi, qi: (bi, qi, pi)),
        scratch_shapes=[pltpu.VMEM((t, LANES), F32)],
        compiler_params=_params(("parallel", "parallel", "arbitrary")),
        name="mla_attention",
    )(q, k, v)


def _sb_kernel(q_ref, k_ref, v_ref, o_ref, acc_ref, *, t):
    qi = pl.program_id(2)
    row = lax.broadcasted_iota(jnp.int32, (t, t), 0)
    col = lax.broadcasted_iota(jnp.int32, (t, t), 1)
    causal = col < row
    after_mat = jnp.where(row > col, 1.0, 0.0).astype(BF16)
    q_all = q_ref[...]
    lane = lax.broadcasted_iota(jnp.int32, q_all.shape, 1)
    first = lane < SB_DIM
    zeros = jnp.zeros_like(q_all)
    qs = (jnp.where(first, q_all, zeros), jnp.where(first, zeros, q_all))
    sign_bit = jnp.uint32(0x80000000)

    def weights(q, k, msk):
        z = _dot_nt(q, k)
        neg_abs = lax.bitcast_convert_type(lax.bitcast_convert_type(z, jnp.uint32) | sign_bit, F32)
        softplus = jnp.maximum(z, 0.0) + jnp.log(1.0 + jnp.exp(neg_abs))
        log_sig = z - softplus
        if msk is not None:
            softplus = jnp.where(msk, softplus, 0.0)
        a = jnp.exp(log_sig - _dot(softplus.astype(BF16), after_mat))
        if msk is not None:
            a = jnp.where(msk, a, 0.0)
        return a.astype(BF16), jnp.sum(softplus, axis=-1, keepdims=True)

    def tile(kj, carry, msk):
        c0, c1 = carry
        start = pl.multiple_of(kj * t, t)
        k = k_ref[pl.ds(start, t), :]
        v = v_ref[pl.ds(start, t), :]
        vz = jnp.zeros_like(v)
        a0, r0 = weights(qs[0], k, msk)
        a1, r1 = weights(qs[1], k, msk)
        pv = _dot(a0, jnp.where(first, v, vz)) + _dot(a1, jnp.where(first, vz, v))
        acc_ref[...] += pv * jnp.where(first, jnp.exp(-c0), jnp.exp(-c1))
        return c0 + r0, c1 + r1

    def live(c0, c1):
        return (jnp.min(jnp.minimum(c0, c1)) < SB_DEAD_CARRY).astype(jnp.int32)

    def step(state):
        it, _, c0, c1 = state
        c0, c1 = tile(qi - 1 - it, (c0, c1), None)
        return it + 1, live(c0, c1), c0, c1

    acc_ref[...] = jnp.zeros_like(acc_ref)
    zero = jnp.zeros((t, 1), F32)
    c0, c1 = tile(qi, (zero, zero), causal)
    lax.while_loop(lambda st: (st[0] < qi) & (st[1] > 0), step, (jnp.int32(0), live(c0, c1), c0, c1))
    o_ref[...] = acc_ref[...].astype(o_ref.dtype)


def _sb_attention(q, k, v, *, t):
    b, s, _ = q.shape
    pairs = SB_HEADS // 2
    qspec = pl.BlockSpec((None, t, LANES), lambda bi, pi, qi: (bi, qi, pi))
    kvspec = pl.BlockSpec((None, s, LANES), lambda bi, pi, qi: (bi, 0, pi))
    return pl.pallas_call(
        functools.partial(_sb_kernel, t=t),
        out_shape=jax.ShapeDtypeStruct((b, s, SB_W), BF16),
        grid=(b, pairs, s // t),
        in_specs=[qspec, kvspec, kvspec],
        out_specs=qspec,
        scratch_shapes=[pltpu.VMEM((t, LANES), F32)],
        compiler_params=_params(("parallel", "parallel", "arbitrary")),
        name="stick_breaking_attention",
    )(q, k, v)


def _outproj_kernel(x_ref, a_ref, b_ref, c_ref, w_ref, o_ref):
    wa = DIFF_W
    wb = wa + MLA_V_W
    o_ref[...] = (x_ref[...] + _dot(a_ref[...], w_ref[0:wa, :]) + _dot(b_ref[...], w_ref[wa:wb, :])
                  + _dot(c_ref[...], w_ref[wb:wb + SB_W, :]))


def _outproj(h, oa, ob, oc, w, *, tm):
    n, d = h.shape
    row = lambda wd: pl.BlockSpec((tm, wd), lambda i: (i, 0))
    return pl.pallas_call(
        _outproj_kernel,
        out_shape=jax.ShapeDtypeStruct((n, d), F32),
        grid=(n // tm,),
        in_specs=[row(d), row(DIFF_W), row(MLA_V_W), row(SB_W), _const_spec(w.shape)],
        out_specs=row(d),
        compiler_params=_params(("parallel",)),
        name="mixer_out_proj",
    )(h, oa, ob, oc, w)


def _rope_tables(rot_dim, seq, lane_starts):
    half = rot_dim // 2
    inv = ROPE_THETA ** (-jnp.arange(0, rot_dim, 2, dtype=F32) / rot_dim)
    ang = jnp.arange(seq, dtype=F32)[:, None] * inv[None, :]
    cos, sin = jnp.cos(ang), jnp.sin(ang)
    pieces_c, pieces_u, pieces_d, lane = [], [], [], 0
    ones = lambda n: jnp.ones((seq, n), F32)
    zeros = lambda n: jnp.zeros((seq, n), F32)
    for st in lane_starts:
        gap = st - lane
        pieces_c += [ones(gap), cos, cos]
        pieces_u += [zeros(gap), -sin, zeros(half)]
        pieces_d += [zeros(gap), zeros(half), sin]
        lane = st + 2 * half
    pieces_c.append(ones(LANES - lane))
    pieces_u.append(zeros(LANES - lane))
    pieces_d.append(zeros(LANES - lane))
    return tuple(jnp.concatenate(p, axis=1) for p in (pieces_c, pieces_u, pieces_d))


def _pack_in_weights(w_in, w_uq, w_ukv):
    d = w_in.shape[0]
    c_rope = DIFF_COLS + MLA_Q_LORA + MLA_KV_LORA
    rope_slab = jnp.zeros((d, LANES), F32).at[:, MLA_NOPE_DIM:MLA_NOPE_DIM + MLA_ROPE_DIM].set(
        w_in[:, c_rope:c_rope + MLA_ROPE_DIM])
    w = jnp.concatenate([w_in[:, :c_rope], rope_slab, w_in[:, c_rope + MLA_ROPE_DIM:]], axis=1)
    qd = MLA_NOPE_DIM + MLA_ROPE_DIM
    uq = jnp.pad(w_uq.reshape(MLA_Q_LORA, MLA_HEADS, qd), ((0, 0), (0, 0), (0, LANES - qd)))
    ukv = w_ukv.reshape(MLA_KV_LORA, MLA_HEADS, MLA_NOPE_DIM + MLA_V_DIM)
    uk = jnp.pad(ukv[:, :, :MLA_NOPE_DIM], ((0, 0), (0, 0), (0, LANES - MLA_NOPE_DIM)))
    uv = ukv[:, :, MLA_NOPE_DIM:]
    ukv_packed = jnp.concatenate([uk.reshape(MLA_KV_LORA, MLA_QK_W), uv.reshape(MLA_KV_LORA, MLA_V_W)], axis=1)
    return w.astype(BF16), uq.reshape(MLA_Q_LORA, MLA_QK_W).astype(BF16), ukv_packed.astype(BF16)


def kernel(x, ffn1_norm, ffn1_w_gate, ffn1_w_up, ffn1_w_down, mix_norm, w_in, diff_lambda_q1, diff_lambda_k1, diff_lambda_q2, diff_lambda_k2, diff_subln, mla_q_norm, mla_w_uq, mla_kv_norm, mla_w_ukv, w_out, ffn2_norm, ffn2_w_gate, ffn2_w_up, ffn2_w_down, final_norm):
    b, s, d = x.shape
    depth = w_in.shape[0]
    tm = min(ROW_TILE, s)
    t_soft = min(SOFTMAX_TILE, s)
    t_sb = min(SB_TILE, s)
    diff_tabs = _rope_tables(DIFF_ROT_DIM, s, (0, DIFF_QK_DIM))
    mla_tabs = _rope_tables(MLA_ROPE_DIM, s, (MLA_NOPE_DIM,))
    fn = final_norm.reshape(1, d)
    h = x.reshape(b * s, d)
    for i in range(depth):
        h = _ffn(h, ffn1_norm[i].reshape(1, d), ffn1_w_gate[i].astype(BF16), ffn1_w_up[i].astype(BF16),
                 ffn1_w_down[i].astype(BF16), fn, tm=tm, final_norm=False)
        w, uq, ukv = _pack_in_weights(w_in[i], mla_w_uq[i], mla_w_ukv[i])
        dq, dk, dv, mq, mk, mv, sq, sk, sv = _inproj(
            h, mix_norm[i].reshape(1, d), w, mla_q_norm[i].reshape(1, -1), mla_kv_norm[i].reshape(1, -1),
            uq, ukv, diff_tabs, mla_tabs, tm=tm, seq=s)
        shape3 = lambda a: a.reshape(b, s, a.shape[-1])
        lam_p = jnp.stack([diff_lambda_q1[i], diff_lambda_k1[i], diff_lambda_q2[i], diff_lambda_k2[i]])
        lambda_init = 0.8 - 0.6 * math.exp(-0.3 * i)
        oa = _diff_attention(lam_p, diff_subln[i].reshape(1, -1), shape3(dq), shape3(dk), shape3(dv),
                             t=t_soft, lambda_init=lambda_init)
        ob = _mla_attention(shape3(mq), shape3(mk), shape3(mv), t=t_soft)
        oc = _sb_attention(shape3(sq), shape3(sk), shape3(sv), t=t_sb)
        flat = lambda a: a.reshape(b * s, a.shape[-1])
        h = _outproj(h, flat(oa), flat(ob), flat(oc), w_out[i].astype(BF16), tm=tm)
        h = _ffn(h, ffn2_norm[i].reshape(1, d), ffn2_w_gate[i].astype(BF16), ffn2_w_up[i].astype(BF16),
                 ffn2_w_down[i].astype(BF16), fn, tm=tm, final_norm=(i == depth - 1))
    return h.reshape(b, s, d)
```

```python
import functools
import math

import jax
import jax.numpy as jnp
from jax import lax
from jax.experimental import pallas as pl
from jax.experimental.pallas import tpu as pltpu

F32 = jnp.float32
BF16 = jnp.bfloat16

D_MODEL = 1024
CHUNK = 64
RMS_EPS = 1e-6
ROPE_THETA = 500000.0

DIFF_HEADS = 4
DIFF_QK_DIM = 64
DIFF_V_DIM = 128
DIFF_ROT_DIM = 16

MLA_HEADS = 4
MLA_Q_LORA = 256
MLA_KV_LORA = 128
MLA_NOPE_DIM = 64
MLA_ROPE_DIM = 32
MLA_V_DIM = 64

SB_HEADS = 4
SB_DIM = 64

LANES = 128
MXU_COLS = 256
DIFF_W = DIFF_HEADS * LANES
MLA_QK_W = MLA_HEADS * LANES
MLA_V_W = MLA_HEADS * MLA_V_DIM
SB_W = SB_HEADS * SB_DIM
DIFF_COLS = 3 * DIFF_W
IN_COLS_PADDED = DIFF_COLS + MLA_Q_LORA + MLA_KV_LORA + LANES + 3 * SB_W

VMEM_LIMIT_BYTES = 56 * 1024 * 1024

ROW_TILE = 512
SOFTMAX_TILE = 1024
SB_TILE = 256
SB_DEAD_CARRY = 104.0


def _rms(x, g):
    return x * lax.rsqrt(jnp.mean(x * x, axis=-1, keepdims=True) + RMS_EPS) * g


def _dot(a, b):
    return jnp.dot(a, b, preferred_element_type=F32)


def _dot_nt(a, b):
    return lax.dot_general(a, b, (((1,), (1,)), ((), ())), preferred_element_type=F32)


def _dot_tn(a, b):
    return lax.dot_general(a, b, (((0,), (0,)), ((), ())), preferred_element_type=F32)


def _const_spec(shape):
    return pl.BlockSpec(shape, lambda *_: (0,) * len(shape), pipeline_mode=pl.Buffered(1))


def _params(semantics):
    return pltpu.CompilerParams(dimension_semantics=semantics, vmem_limit_bytes=VMEM_LIMIT_BYTES)


def _ffn_kernel(x_ref, g_ref, wg_ref, wu_ref, wd_ref, fn_ref, o_ref, *, ff_chunks, final_norm):
    x = x_ref[...]
    hn = _rms(x, g_ref[...]).astype(BF16)
    acc = jnp.zeros(x.shape, F32)
    for c in range(len(ff_chunks) - 1):
        sl = slice(ff_chunks[c], ff_chunks[c + 1])
        gate = _dot(hn, wg_ref[:, sl])
        up = _dot(hn, wu_ref[:, sl])
        act = (gate / (1.0 + jnp.exp(-gate)) * up).astype(BF16)
        acc = acc + _dot(act, wd_ref[sl, :])
    y = x + 0.5 * acc
    if final_norm:
        y = _rms(y, fn_ref[...])
    o_ref[...] = y


def _ffn(h, g, wg, wu, wd, fn, *, tm, final_norm):
    n, d = h.shape
    d_ff = wg.shape[1]
    mid = (d_ff // 2 + MXU_COLS - 1) // MXU_COLS * MXU_COLS
    ff_chunks = (0, mid, d_ff) if 0 < mid < d_ff else (0, d_ff)
    row = pl.BlockSpec((tm, d), lambda i: (i, 0))
    return pl.pallas_call(
        functools.partial(_ffn_kernel, ff_chunks=ff_chunks, final_norm=final_norm),
        out_shape=jax.ShapeDtypeStruct((n, d), F32),
        grid=(n // tm,),
        in_specs=[row, _const_spec((1, d)), _const_spec((d, d_ff)), _const_spec((d, d_ff)),
                  _const_spec((d_ff, d)), _const_spec((1, d))],
        out_specs=row,
        compiler_params=_params(("parallel",)),
        name="swiglu_half_step",
    )(h, g, wg, wu, wd, fn)


def _rope(x, cos, sin_up, sin_down, half):
    n = x.shape[-1]
    return x * cos + pltpu.roll(x, n - half, 1) * sin_up + pltpu.roll(x, half, 1) * sin_down


def _inproj_kernel(x_ref, g_ref, w_ref, qn_ref, kvn_ref, wuq_ref, wukv_ref,
                   dc_ref, du_ref, dd_ref, mc_ref, mu_ref, md_ref,
                   dq_ref, dk_ref, dv_ref, mq_ref, mk_ref, mv_ref, sq_ref, sk_ref, sv_ref):
    hn = _rms(x_ref[...], g_ref[...]).astype(BF16)
    dc, du, dd = dc_ref[...], du_ref[...], dd_ref[...]
    mc, mu, md = mc_ref[...], mu_ref[...], md_ref[...]
    d_half = DIFF_ROT_DIM // 2
    m_half = MLA_ROPE_DIM // 2

    q_scale = DIFF_QK_DIM ** -0.5
    q = _dot(hn, w_ref[:, 0:DIFF_W])
    k = _dot(hn, w_ref[:, DIFF_W:2 * DIFF_W])
    for hd in range(DIFF_HEADS):
        sl = slice(hd * LANES, (hd + 1) * LANES)
        dq_ref[:, sl] = (_rope(q[:, sl], dc, du, dd, d_half) * q_scale).astype(BF16)
        dk_ref[:, sl] = _rope(k[:, sl], dc, du, dd, d_half).astype(BF16)
    dv_ref[...] = _dot(hn, w_ref[:, 2 * DIFF_W:3 * DIFF_W]).astype(BF16)

    mla_scale = (MLA_NOPE_DIM + MLA_ROPE_DIM) ** -0.5
    c0 = DIFF_COLS
    c_q = _rms(_dot(hn, w_ref[:, c0:c0 + MLA_Q_LORA]), qn_ref[...]).astype(BF16)
    c0 += MLA_Q_LORA
    kv = _dot(hn, w_ref[:, c0:c0 + MLA_KV_LORA + LANES])
    c_kv = _rms(kv[:, 0:MLA_KV_LORA], kvn_ref[...]).astype(BF16)
    k_rope = _rope(kv[:, MLA_KV_LORA:], mc, mu, md, m_half)
    c0 += MLA_KV_LORA + LANES
    q = _dot(c_q, wuq_ref[...])
    up = _dot(c_kv, wukv_ref[...])
    for hd in range(MLA_HEADS):
        sl = slice(hd * LANES, (hd + 1) * LANES)
        mq_ref[:, sl] = (_rope(q[:, sl], mc, mu, md, m_half) * mla_scale).astype(BF16)
        mk_ref[:, sl] = (up[:, sl] + k_rope).astype(BF16)
    mv_ref[...] = up[:, MLA_QK_W:].astype(BF16)

    sq_ref[...] = (_dot(hn, w_ref[:, c0:c0 + SB_W]) * (SB_DIM ** -0.5)).astype(BF16)
    sk_ref[...] = _dot(hn, w_ref[:, c0 + SB_W:c0 + 2 * SB_W]).astype(BF16)
    sv_ref[...] = _dot(hn, w_ref[:, c0 + 2 * SB_W:c0 + 3 * SB_W]).astype(BF16)


def _inproj(h, g, w, qn, kvn, wuq, wukv, diff_tabs, mla_tabs, *, tm, seq):
    n, d = h.shape
    pos_blocks = seq // tm
    row = pl.BlockSpec((tm, d), lambda i: (i, 0))
    tab = pl.BlockSpec((tm, LANES), lambda i: (i % pos_blocks, 0))
    widths = (DIFF_W, DIFF_W, DIFF_W, MLA_QK_W, MLA_QK_W, MLA_V_W, SB_W, SB_W, SB_W)
    return pl.pallas_call(
        _inproj_kernel,
        out_shape=tuple(jax.ShapeDtypeStruct((n, wd), BF16) for wd in widths),
        grid=(n // tm,),
        in_specs=[row, _const_spec((1, d)), _const_spec(w.shape), _const_spec(qn.shape),
                  _const_spec(kvn.shape), _const_spec(wuq.shape), _const_spec(wukv.shape)] + [tab] * 6,
        out_specs=tuple(pl.BlockSpec((tm, wd), lambda i: (i, 0)) for wd in widths),
        compiler_params=_params(("parallel",)),
        name="mixer_in_proj",
    )(h, g, w, qn, kvn, wuq, wukv, *diff_tabs, *mla_tabs)


def _chunk_mask_t(t):
    key = lax.broadcasted_iota(jnp.int32, (t, t), 0)
    qry = lax.broadcasted_iota(jnp.int32, (t, t), 1)
    return (key // CHUNK) <= (qry // CHUNK)


def _softmax_update_t(s, m, l, acc_ref, v):
    m_new = jnp.maximum(m, jnp.max(s, axis=0, keepdims=True))
    alpha = jnp.exp(m - m_new)
    p = jnp.exp(s - m_new)
    l_new = alpha * l + jnp.sum(p, axis=0, keepdims=True)
    acc_ref[...] = alpha * acc_ref[...] + _dot_tn(v, p.astype(BF16))
    return m_new, l_new


def _diff_kernel(lam_ref, subln_ref, q_ref, k_ref, v_ref, o_ref, acc1_ref, acc2_ref, *, t, lambda_init):
    qi = pl.program_id(2)
    q = q_ref[...]
    lane = lax.broadcasted_iota(jnp.int32, q.shape, 1)
    q1 = jnp.where(lane < DIFF_QK_DIM, q, jnp.zeros_like(q))
    q2 = jnp.where(lane >= DIFF_QK_DIM, q, jnp.zeros_like(q))
    acc1_ref[...] = jnp.zeros_like(acc1_ref)
    acc2_ref[...] = jnp.zeros_like(acc2_ref)

    def tile(kj, carry, mask):
        m1, l1, m2, l2 = carry
        start = pl.multiple_of(kj * t, t)
        k = k_ref[pl.ds(start, t), :]
        v = v_ref[pl.ds(start, t), :]
        s1 = _dot_nt(k, q1)
        s2 = _dot_nt(k, q2)
        if mask is not None:
            s1 = jnp.where(mask, s1, -jnp.inf)
            s2 = jnp.where(mask, s2, -jnp.inf)
        m1, l1 = _softmax_update_t(s1, m1, l1, acc1_ref, v)
        m2, l2 = _softmax_update_t(s2, m2, l2, acc2_ref, v)
        return m1, l1, m2, l2

    neg = jnp.full((1, t), -jnp.inf, F32)
    zero = jnp.zeros((1, t), F32)
    carry = lax.fori_loop(0, qi, lambda kj, c: tile(kj, c, None), (neg, zero, neg, zero))
    _, l1, _, l2 = tile(qi, carry, _chunk_mask_t(t))

    lam_p = lam_ref[...]
    lam = (jnp.exp(jnp.sum(lam_p[0:1] * lam_p[1:2], axis=-1, keepdims=True))
           - jnp.exp(jnp.sum(lam_p[2:3] * lam_p[3:4], axis=-1, keepdims=True)) + lambda_init)
    o = acc1_ref[...] / l1 - lam * (acc2_ref[...] / l2)
    o = o * lax.rsqrt(jnp.mean(o * o, axis=0, keepdims=True) + RMS_EPS) * subln_ref[...]
    o_ref[...] = (o * (1.0 - lambda_init)).T.astype(o_ref.dtype)


def _diff_attention(lam_p, subln, q, k, v, *, t, lambda_init):
    b, s, _ = q.shape
    qspec = pl.BlockSpec((None, t, LANES), lambda bi, hi, qi: (bi, qi, hi))
    kvspec = pl.BlockSpec((None, s, LANES), lambda bi, hi, qi: (bi, 0, hi))
    return pl.pallas_call(
        functools.partial(_diff_kernel, t=t, lambda_init=lambda_init),
        out_shape=jax.ShapeDtypeStruct((b, s, DIFF_W), BF16),
        grid=(b, DIFF_HEADS, s // t),
        in_specs=[_const_spec(lam_p.shape), _const_spec(subln.shape), qspec, kvspec, kvspec],
        out_specs=qspec,
        scratch_shapes=[pltpu.VMEM((LANES, t), F32)] * 2,
        compiler_params=_params(("parallel", "parallel", "arbitrary")),
        name="diff_attention",
    )(lam_p, subln, q, k, v)


def _mla_kernel(q_ref, k_ref, v_ref, o_ref, acc_ref, *, t):
    qi = pl.program_id(2)
    q0 = q_ref[:, 0:LANES]
    q1 = q_ref[:, LANES:2 * LANES]
    rows = (slice(0, MLA_V_DIM), slice(MLA_V_DIM, 2 * MLA_V_DIM))

    def tile(kj, carry, msk):
        m0, l0, m1, l1 = carry
        start = pl.multiple_of(kj * t, t)
        s0 = _dot_nt(k_ref[pl.ds(start, t), 0:LANES], q0)
        s1 = _dot_nt(k_ref[pl.ds(start, t), LANES:2 * LANES], q1)
        if msk is not None:
            s0 = jnp.where(msk, s0, -jnp.inf)
            s1 = jnp.where(msk, s1, -jnp.inf)
        v = v_ref[pl.ds(start, t), :]
        out = []
        for s, m, l, r in ((s0, m0, l0, rows[0]), (s1, m1, l1, rows[1])):
            m_new = jnp.maximum(m, jnp.max(s, axis=0, keepdims=True))
            alpha = jnp.exp(m - m_new)
            p = jnp.exp(s - m_new)
            l_new = alpha * l + jnp.sum(p, axis=0, keepdims=True)
            acc_ref[r, :] = alpha * acc_ref[r, :] + _dot_tn(v, p.astype(BF16))[r, :]
            out += [m_new, l_new]
        return tuple(out)

    acc_ref[...] = jnp.zeros_like(acc_ref)
    neg = jnp.full((1, t), -jnp.inf, F32)
    zero = jnp.zeros((1, t), F32)
    carry = lax.fori_loop(0, qi, lambda kj, c: tile(kj, c, None), (neg, zero, neg, zero))
    _, l0, _, l1 = tile(qi, carry, _chunk_mask_t(t))
    acc_ref[rows[0], :] = acc_ref[rows[0], :] / l0
    acc_ref[rows[1], :] = acc_ref[rows[1], :] / l1
    o_ref[...] = acc_ref[...].T.astype(o_ref.dtype)


def _mla_attention(q, k, v, *, t):
    b, s, _ = q.shape
    pairs = MLA_HEADS // 2
    return pl.pallas_call(
        functools.partial(_mla_kernel, t=t),
        out_shape=jax.ShapeDtypeStruct((b, s, MLA_V_W), BF16),
        grid=(b, pairs, s // t),
        in_specs=[pl.BlockSpec((None, t, 2 * LANES), lambda bi, pi, qi: (bi, qi, pi)),
                  pl.BlockSpec((None, s, 2 * LANES), lambda bi, pi, qi: (bi, 0, pi)),
                  pl.BlockSpec((None, s, LANES), lambda bi, pi, qi: (bi, 0, pi))],
        out_specs=pl.BlockSpec((None, t, LANES), lambda bi, pi, qi: (bi, qi, pi)),
        scratch_shapes=[pltpu.VMEM((LANES, t), F32)],
        compiler_params=_params(("parallel", "parallel", "arbitrary")),
        name="mla_attention",
    )(q, k, v)


def _sb_kernel(q_ref, k_ref, v_ref, o_ref, acc_ref, *, t):
    qi = pl.program_id(2)
    row = lax.broadcasted_iota(jnp.int32, (t, t), 0)
    col = lax.broadcasted_iota(jnp.int32, (t, t), 1)
    causal = col < row
    after_mat = jnp.where(row > col, 1.0, 0.0).astype(BF16)
    q_all = q_ref[...]
    lane = lax.broadcasted_iota(jnp.int32, q_all.shape, 1)
    first = lane < SB_DIM
    zeros = jnp.zeros_like(q_all)
    qs = (jnp.where(first, q_all, zeros), jnp.where(first, zeros, q_all))
    sign_bit = jnp.uint32(0x80000000)

    def weights(q, k, msk):
        z = _dot_nt(q, k)
        neg_abs = lax.bitcast_convert_type(lax.bitcast_convert_type(z, jnp.uint32) | sign_bit, F32)
        softplus = jnp.maximum(z, 0.0) + jnp.log(1.0 + jnp.exp(neg_abs))
        log_sig = z - softplus
        if msk is not None:
            softplus = jnp.where(msk, softplus, 0.0)
        a = jnp.exp(log_sig - _dot(softplus.astype(BF16), after_mat))
        if msk is not None:
            a = jnp.where(msk, a, 0.0)
        return a.astype(BF16), jnp.sum(softplus, axis=-1, keepdims=True)

    def tile(kj, carry, msk):
        c0, c1 = carry
        start = pl.multiple_of(kj * t, t)
        k = k_ref[pl.ds(start, t), :]
        v = v_ref[pl.ds(start, t), :]
        vz = jnp.zeros_like(v)
        a0, r0 = weights(qs[0], k, msk)
        a1, r1 = weights(qs[1], k, msk)
        pv = _dot(a0, jnp.where(first, v, vz)) + _dot(a1, jnp.where(first, vz, v))
        acc_ref[...] += pv * jnp.where(first, jnp.exp(-c0), jnp.exp(-c1))
        return c0 + r0, c1 + r1

    def live(c0, c1):
        return (jnp.min(jnp.minimum(c0, c1)) < SB_DEAD_CARRY).astype(jnp.int32)

    def step(state):
        it, _, c0, c1 = state
        c0, c1 = tile(qi - 1 - it, (c0, c1), None)
        return it + 1, live(c0, c1), c0, c1

    acc_ref[...] = jnp.zeros_like(acc_ref)
    zero = jnp.zeros((t, 1), F32)
    c0, c1 = tile(qi, (zero, zero), causal)
    lax.while_loop(lambda st: (st[0] < qi) & (st[1] > 0), step, (jnp.int32(0), live(c0, c1), c0, c1))
    o_ref[...] = acc_ref[...].astype(o_ref.dtype)


def _sb_attention(q, k, v, *, t):
    b, s, _ = q.shape
    pairs = SB_HEADS // 2
    qspec = pl.BlockSpec((None, t, LANES), lambda bi, pi, qi: (bi, qi, pi))
    kvspec = pl.BlockSpec((None, s, LANES), lambda bi, pi, qi: (bi, 0, pi))
    return pl.pallas_call(
        functools.partial(_sb_kernel, t=t),
        out_shape=jax.ShapeDtypeStruct((b, s, SB_W), BF16),
        grid=(b, pairs, s // t),
        in_specs=[qspec, kvspec, kvspec],
        out_specs=qspec,
        scratch_shapes=[pltpu.VMEM((t, LANES), F32)],
        compiler_params=_params(("parallel", "parallel", "arbitrary")),
        name="stick_breaking_attention",
    )(q, k, v)


def _outproj_kernel(x_ref, a_ref, b_ref, c_ref, w_ref, o_ref):
    wa = DIFF_W
    wb = wa + MLA_V_W
    o_ref[...] = (x_ref[...] + _dot(a_ref[...], w_ref[0:wa, :]) + _dot(b_ref[...], w_ref[wa:wb, :])
                  + _dot(c_ref[...], w_ref[wb:wb + SB_W, :]))


def _outproj(h, oa, ob, oc, w, *, tm):
    n, d = h.shape
    row = lambda wd: pl.BlockSpec((tm, wd), lambda i: (i, 0))
    return pl.pallas_call(
        _outproj_kernel,
        out_shape=jax.ShapeDtypeStruct((n, d), F32),
        grid=(n // tm,),
        in_specs=[row(d), row(DIFF_W), row(MLA_V_W), row(SB_W), _const_spec(w.shape)],
        out_specs=row(d),
        compiler_params=_params(("parallel",)),
        name="mixer_out_proj",
    )(h, oa, ob, oc, w)


def _rope_tables(rot_dim, seq, lane_starts):
    half = rot_dim // 2
    inv = ROPE_THETA ** (-jnp.arange(0, rot_dim, 2, dtype=F32) / rot_dim)
    ang = jnp.arange(seq, dtype=F32)[:, None] * inv[None, :]
    cos, sin = jnp.cos(ang), jnp.sin(ang)
    pieces_c, pieces_u, pieces_d, lane = [], [], [], 0
    ones = lambda n: jnp.ones((seq, n), F32)
    zeros = lambda n: jnp.zeros((seq, n), F32)
    for st in lane_starts:
        gap = st - lane
        pieces_c += [ones(gap), cos, cos]
        pieces_u += [zeros(gap), -sin, zeros(half)]
        pieces_d += [zeros(gap), zeros(half), sin]
        lane = st + 2 * half
    pieces_c.append(ones(LANES - lane))
    pieces_u.append(zeros(LANES - lane))
    pieces_d.append(zeros(LANES - lane))
    return tuple(jnp.concatenate(p, axis=1) for p in (pieces_c, pieces_u, pieces_d))


def _pack_in_weights(w_in, w_uq, w_ukv):
    d = w_in.shape[0]
    c_rope = DIFF_COLS + MLA_Q_LORA + MLA_KV_LORA
    rope_slab = jnp.zeros((d, LANES), F32).at[:, MLA_NOPE_DIM:MLA_NOPE_DIM + MLA_ROPE_DIM].set(
        w_in[:, c_rope:c_rope + MLA_ROPE_DIM])
    w = jnp.concatenate([w_in[:, :c_rope], rope_slab, w_in[:, c_rope + MLA_ROPE_DIM:]], axis=1)
    qd = MLA_NOPE_DIM + MLA_ROPE_DIM
    uq = jnp.pad(w_uq.reshape(MLA_Q_LORA, MLA_HEADS, qd), ((0, 0), (0, 0), (0, LANES - qd)))
    ukv = w_ukv.reshape(MLA_KV_LORA, MLA_HEADS, MLA_NOPE_DIM + MLA_V_DIM)
    uk = jnp.pad(ukv[:, :, :MLA_NOPE_DIM], ((0, 0), (0, 0), (0, LANES - MLA_NOPE_DIM)))
    uv = ukv[:, :, MLA_NOPE_DIM:]
    ukv_packed = jnp.concatenate([uk.reshape(MLA_KV_LORA, MLA_QK_W), uv.reshape(MLA_KV_LORA, MLA_V_W)], axis=1)
    return w.astype(BF16), uq.reshape(MLA_Q_LORA, MLA_QK_W).astype(BF16), ukv_packed.astype(BF16)


def kernel(x, ffn1_norm, ffn1_w_gate, ffn1_w_up, ffn1_w_down, mix_norm, w_in, diff_lambda_q1, diff_lambda_k1, diff_lambda_q2, diff_lambda_k2, diff_subln, mla_q_norm, mla_w_uq, mla_kv_norm, mla_w_ukv, w_out, ffn2_norm, ffn2_w_gate, ffn2_w_up, ffn2_w_down, final_norm):
    b, s, d = x.shape
    depth = w_in.shape[0]
    tm = min(ROW_TILE, s)
    t_soft = min(SOFTMAX_TILE, s)
    t_sb = min(SB_TILE, s)
    diff_tabs = _rope_tables(DIFF_ROT_DIM, s, (0, DIFF_QK_DIM))
    mla_tabs = _rope_tables(MLA_ROPE_DIM, s, (MLA_NOPE_DIM,))
    fn = final_norm.reshape(1, d)
    h = x.reshape(b * s, d)
    for i in range(depth):
        h = _ffn(h, ffn1_norm[i].reshape(1, d), ffn1_w_gate[i].astype(BF16), ffn1_w_up[i].astype(BF16),
                 ffn1_w_down[i].astype(BF16), fn, tm=tm, final_norm=False)
        w, uq, ukv = _pack_in_weights(w_in[i], mla_w_uq[i], mla_w_ukv[i])
        dq, dk, dv, mq, mk, mv, sq, sk, sv = _inproj(
            h, mix_norm[i].reshape(1, d), w, mla_q_norm[i].reshape(1, -1), mla_kv_norm[i].reshape(1, -1),
            uq, ukv, diff_tabs, mla_tabs, tm=tm, seq=s)
        shape3 = lambda a: a.reshape(b, s, a.shape[-1])
        lam_p = jnp.stack([diff_lambda_q1[i], diff_lambda_k1[i], diff_lambda_q2[i], diff_lambda_k2[i]])
        lambda_init = 0.8 - 0.6 * math.exp(-0.3 * i)
        oa = _diff_attention(lam_p, diff_subln[i].reshape(-1, 1), shape3(dq), shape3(dk), shape3(dv),
                             t=t_soft, lambda_init=lambda_init)
        ob = _mla_attention(shape3(mq), shape3(mk), shape3(mv), t=t_soft)
        oc = _sb_attention(shape3(sq), shape3(sk), shape3(sv), t=t_sb)
        flat = lambda a: a.reshape(b * s, a.shape[-1])
        h = _outproj(h, flat(oa), flat(ob), flat(oc), w_out[i].astype(BF16), tm=tm)
        h = _ffn(h, ffn2_norm[i].reshape(1, d), ffn2_w_gate[i].astype(BF16), ffn2_w_up[i].astype(BF16),
                 ffn2_w_down[i].astype(BF16), fn, tm=tm, final_norm=(i == depth - 1))
    return h.reshape(b, s, d)
```
